```python
import jax, jax.numpy as jnp
from jax import lax
import numpy as np

D_MODEL = 4096
BATCH = 1
SEQ = 8192
DEPTH = 1
DEC_BATCH = 32
DEC_SEQ = 32
PAST_LEN = 1024

CHUNK = 64
PLE_DIM = 256
HEAD_SIZE = 64
RWKV_WIDTH = D_MODEL
RWKV_HEADS = RWKV_WIDTH // HEAD_SIZE
DECAY_LORA = max(32, int(round(1.8 * D_MODEL ** 0.5 / 32)) * 32)
AAA_LORA = max(32, int(round(1.8 * D_MODEL ** 0.5 / 32)) * 32)
GATE_LORA = max(32, int(round(0.6 * D_MODEL ** 0.8 / 32)) * 32)
CONV_WIDTH = D_MODEL
CONV_K = 31
D_FF = 4 * D_MODEL
N_SHIFT = 3 * RWKV_WIDTH + DECAY_LORA + AAA_LORA + GATE_LORA
N_IN = N_SHIFT + 2 * CONV_WIDTH + 2 * D_MODEL
RMS_EPS = 1e-6
LN_EPS = 1e-5
GN_EPS = 64e-5

kernel_name = 'rwkv7_conformer_gated_stream_step'

f32 = jnp.float32


def _rms(x, g):
    xf = x.astype(f32)
    y = xf * lax.rsqrt(jnp.mean(xf * xf, axis=-1, keepdims=True) + RMS_EPS)
    return (y * g.astype(f32)).astype(x.dtype)


def _ln(x, g, b):
    xf = x.astype(f32)
    mu = jnp.mean(xf, axis=-1, keepdims=True)
    var = jnp.mean(jnp.square(xf - mu), axis=-1, keepdims=True)
    y = (xf - mu) * lax.rsqrt(var + LN_EPS)
    return (y * g.astype(f32) + b.astype(f32)).astype(x.dtype)


def _wkv_scan(r, w, k, v, aa, bb, s0):
    def step(s, inp):
        r_t, w_t, k_t, v_t, aa_t, bb_t = inp
        sa = jnp.einsum('bhij,bhj->bhi', s, aa_t)
        s = s * w_t[:, :, None, :] + sa[..., None] * bb_t[:, :, None, :] + v_t[..., None] * k_t[:, :, None, :]
        return s, jnp.einsum('bhij,bhj->bhi', s, r_t)
    seq = tuple(jnp.moveaxis(t, 1, 0) for t in (r, w, k, v, aa, bb))
    s_T, ys = lax.scan(step, s0, seq)
    return jnp.moveaxis(ys, 0, 1), s_T


def _layer(x, p, s_wkv, s_shift, s_conv,
           g_pre_mix, w_in, mu_shift, w0, w2, a0, a2, g2, k_k, k_a, r_k, lnx_g, lnx_b, w_o_a,
           dw_w, dw_b, lnc_g, lnc_b, w_o_b, w_out, g_post_mix,
           g_pre_ffn, w_up, w_down, g_post_ffn,
           g_pre_ple, w_ple_gate, w_ple, g_post_ple):
    dt = x.dtype
    B, T, _ = x.shape
    H, N = RWKV_HEADS, HEAD_SIZE
    h = _rms(x, g_pre_mix)
    proj = jnp.einsum('btd,dn->btn', h, w_in)
    pa = proj[..., :N_SHIFT]
    pc = proj[..., N_SHIFT:N_SHIFT + 2 * CONV_WIDTH]
    pg = proj[..., N_SHIFT + 2 * CONV_WIDTH:]

    prev = jnp.concatenate([s_shift[:, None, :].astype(dt), pa[:, :-1]], axis=1)
    xs = (pa + (prev - pa) * mu_shift).astype(f32)
    c = RWKV_WIDTH
    r = xs[..., :c]
    k = xs[..., c:2 * c]
    v = xs[..., 2 * c:3 * c]
    o = 3 * c
    wl = xs[..., o:o + DECAY_LORA]
    o = o + DECAY_LORA
    al = xs[..., o:o + AAA_LORA]
    o = o + AAA_LORA
    gl = xs[..., o:o + GATE_LORA]
    wraw = -jax.nn.softplus(-(w0.astype(f32) + jnp.tanh(wl) @ w2.astype(f32))) - 0.5
    decay = jnp.exp(-jnp.exp(wraw))
    a = jax.nn.sigmoid(a0.astype(f32) + al @ a2.astype(f32))
    g = jax.nn.sigmoid(gl) @ g2.astype(f32)
    hd = lambda t: t.reshape(B, T, H, N)
    r, k, v, decay, a = hd(r), hd(k), hd(v), hd(decay), hd(a)
    kk = k * k_k.reshape(H, N).astype(f32)
    kk = kk / jnp.maximum(jnp.sqrt(jnp.sum(kk * kk, axis=-1, keepdims=True)), 1e-12)
    k = k * (1.0 + (a - 1.0) * k_a.reshape(H, N).astype(f32))
    y, s_new = _wkv_scan(r, decay, k, v, -kk, kk * a, s_wkv.astype(f32))
    mu = jnp.mean(y, axis=-1, keepdims=True)
    var = jnp.mean(jnp.square(y - mu), axis=-1, keepdims=True)
    y = ((y - mu) * lax.rsqrt(var + GN_EPS)).reshape(B, T, c) * lnx_g.astype(f32) + lnx_b.astype(f32)
    bonus = (jnp.sum(r * k * r_k.astype(f32), axis=-1, keepdims=True) * v).reshape(B, T, c)
    y_a = jnp.einsum('btc,cd->btd', ((y + bonus) * g).astype(dt), w_o_a)

    u = pc[..., :CONV_WIDTH] * jax.nn.sigmoid(pc[..., CONV_WIDTH:])
    u_full = jnp.concatenate([s_conv.astype(dt), u], axis=1)
    new_conv = u_full[:, -(CONV_K - 1):]
    cv = lax.conv_general_dilated(u_full, dw_w[:, None, :].astype(dt), (1,), 'VALID',
                                  dimension_numbers=('NWC', 'WIO', 'NWC'),
                                  feature_group_count=CONV_WIDTH) + dw_b
    cv = jax.nn.silu(_ln(cv, lnc_g, lnc_b))
    y_b = jnp.einsum('btc,cd->btd', cv, w_o_b)

    gate_a = jax.nn.sigmoid(pg[..., :D_MODEL])
    gate_b = jax.nn.sigmoid(pg[..., D_MODEL:])
    m = jnp.einsum('btd,de->bte', gate_a * y_a + gate_b * y_b, w_out)
    x = x + _rms(m, g_post_mix)

    hf = _rms(x, g_pre_ffn)
    f = jnp.square(jax.nn.relu(jnp.einsum('btd,df->btf', hf, w_up)))
    f = jnp.einsum('btf,fd->btd', f, w_down)
    x = x + _rms(f, g_post_ffn)

    pe = jnp.einsum('btq,qd->btd', p, w_ple)
    pgate = jax.nn.sigmoid(jnp.einsum('btd,de->bte', _rms(x, g_pre_ple), w_ple_gate))
    x = x + _rms(pe * pgate, g_post_ple)
    return x, s_new.astype(dt), pa[:, -1], new_conv


def _normal(key, shape, scale):
    return jax.random.normal(key, shape, f32) * scale


def setup_inputs(seed: int = 0) -> dict:
    key = jax.random.key(seed)
    ks = jax.random.split(key, 40)
    L, D, C, CB = DEPTH, D_MODEL, RWKV_WIDTH, CONV_WIDTH
    gain = lambda kk, n: 1.0 + _normal(kk, (L, n), 0.02)
    return {
        'x_prompt': _normal(ks[0], (BATCH, SEQ, D), 1.0),
        'x_sample': _normal(ks[1], (DEC_BATCH, DEC_SEQ, D), 1.0),
        'p_prompt': _normal(ks[2], (L, BATCH, SEQ, PLE_DIM), 1.0),
        'p_sample': _normal(ks[3], (L, DEC_BATCH, DEC_SEQ, PLE_DIM), 1.0),
        'state_wkv': _normal(ks[4], (L, DEC_BATCH, RWKV_HEADS, HEAD_SIZE, HEAD_SIZE), 0.1),
        'state_shift': _normal(ks[5], (L, DEC_BATCH, N_SHIFT), 1.0),
        'state_conv': _normal(ks[6], (L, DEC_BATCH, CONV_K - 1, CB), 0.5),
        'g_pre_mix': gain(ks[7], D),
        'w_in': _normal(ks[8], (L, D, N_IN), D ** -0.5),
        'mu_shift': jax.random.uniform(ks[9], (L, N_SHIFT), f32),
        'w0': jax.random.uniform(ks[10], (L, C), f32, -5.0, 1.0),
        'w2': _normal(ks[11], (L, DECAY_LORA, C), 0.1 * DECAY_LORA ** -0.5),
        'a0': _normal(ks[12], (L, C), 0.1),
        'a2': _normal(ks[13], (L, AAA_LORA, C), AAA_LORA ** -0.5),
        'g2': _normal(ks[14], (L, GATE_LORA, C), GATE_LORA ** -0.5),
        'k_k': 0.85 + _normal(ks[15], (L, C), 0.02),
        'k_a': 1.0 + _normal(ks[16], (L, C), 0.02),
        'r_k': _normal(ks[17], (L, RWKV_HEADS, HEAD_SIZE), 0.1),
        'lnx_g': gain(ks[18], C),
        'lnx_b': _normal(ks[19], (L, C), 0.01),
        'w_o_a': _normal(ks[20], (L, C, D), C ** -0.5),
        'dw_w': _normal(ks[21], (L, CONV_K, CB), CONV_K ** -0.5),
        'dw_b': _normal(ks[22], (L, CB), 0.01),
        'lnc_g': gain(ks[23], CB),
        'lnc_b': _normal(ks[24], (L, CB), 0.01),
        'w_o_b': _normal(ks[25], (L, CB, D), CB ** -0.5),
        'w_out': _normal(ks[26], (L, D, D), D ** -0.5),
        'g_post_mix': gain(ks[27], D),
        'g_pre_ffn': gain(ks[28], D),
        'w_up': _normal(ks[29], (L, D, D_FF), D ** -0.5),
        'w_down': _normal(ks[30], (L, D_FF, D), D_FF ** -0.5),
        'g_post_ffn': gain(ks[31], D),
        'g_pre_ple': gain(ks[32], D),
        'w_ple_gate': _normal(ks[33], (L, D, D), D ** -0.5),
        'w_ple': _normal(ks[34], (L, PLE_DIM, D), PLE_DIM ** -0.5),
        'g_post_ple': gain(ks[35], D),
    }


def reference(x_prompt, x_sample, p_prompt, p_sample, state_wkv, state_shift, state_conv,
              g_pre_mix, w_in, mu_shift, w0, w2, a0, a2, g2, k_k, k_a, r_k, lnx_g, lnx_b, w_o_a,
              dw_w, dw_b, lnc_g, lnc_b, w_o_b, w_out, g_post_mix,
              g_pre_ffn, w_up, w_down, g_post_ffn,
              g_pre_ple, w_ple_gate, w_ple, g_post_ple):
    dt = x_prompt.dtype
    bp = x_prompt.shape[0]
    xp, xs = x_prompt, x_sample
    wkv_p, shift_p, conv_p, wkv_s, shift_s, conv_s = [], [], [], [], [], []
    for i in range(DEPTH):
        lw = (g_pre_mix[i], w_in[i], mu_shift[i], w0[i], w2[i], a0[i], a2[i], g2[i], k_k[i], k_a[i],
              r_k[i], lnx_g[i], lnx_b[i], w_o_a[i], dw_w[i], dw_b[i], lnc_g[i], lnc_b[i], w_o_b[i],
              w_out[i], g_post_mix[i], g_pre_ffn[i], w_up[i], w_down[i], g_post_ffn[i],
              g_pre_ple[i], w_ple_gate[i], w_ple[i], g_post_ple[i])
        xp, s1, s2, s3 = _layer(xp, p_prompt[i],
                                jnp.zeros((bp, RWKV_HEADS, HEAD_SIZE, HEAD_SIZE), f32),
                                jnp.zeros((bp, N_SHIFT), dt),
                                jnp.zeros((bp, CONV_K - 1, CONV_WIDTH), dt), *lw)
        xs, t1, t2, t3 = _layer(xs, p_sample[i], state_wkv[i], state_shift[i], state_conv[i], *lw)
        wkv_p.append(s1)
        shift_p.append(s2)
        conv_p.append(s3)
        wkv_s.append(t1)
        shift_s.append(t2)
        conv_s.append(t3)
    return (xp, xs, jnp.stack(wkv_p), jnp.stack(shift_p), jnp.stack(conv_p),
            jnp.stack(wkv_s), jnp.stack(shift_s), jnp.stack(conv_s))
```

```python
import functools

import jax
import jax.numpy as jnp
from jax import lax
from jax.experimental import pallas as pl
from jax.experimental.pallas import tpu as pltpu

f32 = jnp.float32
bf16 = jnp.bfloat16

HEAD_SIZE = 64
CONV_K = 31
CONV_HALO = 32
RMS_EPS = 1e-6
LN_EPS = 1e-5
GN_EPS = 64e-5
KK_EPS = 1e-12
VMEM_LIMIT = 56 * 1024 * 1024

_NN = (((1,), (0,)), ((), ()))
_NT = (((1,), (1,)), ((), ()))
_TN = (((0,), (0,)), ((), ()))


def _params(*sem):
    return pltpu.CompilerParams(dimension_semantics=sem, vmem_limit_bytes=VMEM_LIMIT)


def _tile(n, pref):
    t = min(n, pref)
    while n % t:
        t -= 1
    return t


def _dot(a, b, dims=_NN):
    return lax.dot_general(a.astype(bf16), b.astype(bf16), dims, preferred_element_type=f32)


def _split(a):
    hi = a.astype(bf16)
    lo = (a - hi.astype(f32)).astype(bf16)
    return hi, lo


def _dot3(a, b, dims=_NN):
    a_hi, a_lo = _split(a)
    b_hi, b_lo = _split(b)
    d = functools.partial(lax.dot_general, dimension_numbers=dims, preferred_element_type=f32)
    return d(a_hi, b_hi) + (d(a_hi, b_lo) + d(a_lo, b_hi))


def _dot_exact_rhs(a, b_bf16, dims=_NN):
    a_hi, a_lo = _split(a)
    d = functools.partial(lax.dot_general, dimension_numbers=dims, preferred_element_type=f32)
    return d(a_hi, b_bf16) + d(a_lo, b_bf16)


def _sigmoid(x):
    return 1.0 / (1.0 + jnp.exp(-x))


def _rms(x, g):
    return x * lax.rsqrt(jnp.mean(x * x, axis=-1, keepdims=True) + RMS_EPS) * g


def _rms_cast_kernel(x_ref, g_ref, o_ref):
    o_ref[...] = _rms(x_ref[...], g_ref[...]).astype(o_ref.dtype)


def _rms_cast(x, g):
    m, d = x.shape
    tm = _tile(m, 512)
    return pl.pallas_call(
        _rms_cast_kernel,
        out_shape=jax.ShapeDtypeStruct((m, d), bf16),
        grid=(m // tm,),
        in_specs=[pl.BlockSpec((tm, d), lambda i: (i, 0)), pl.BlockSpec((1, d), lambda i: (0, 0))],
        out_specs=pl.BlockSpec((tm, d), lambda i: (i, 0)),
        compiler_params=_params("parallel"),
        name="rms_cast",
    )(x, g)


def _mm_kernel(a_ref, b_ref, o_ref, *, act):
    acc = jnp.dot(a_ref[...], b_ref[...], preferred_element_type=f32)
    if act == "sigmoid":
        acc = _sigmoid(acc)
    o_ref[...] = acc.astype(o_ref.dtype)


def _matmul(a, b, act=None, out_dtype=f32, tm_pref=512, tn_pref=1024, name="mm"):
    m, k = a.shape
    n = b.shape[1]
    tm, tn = _tile(m, tm_pref), _tile(n, tn_pref)
    return pl.pallas_call(
        functools.partial(_mm_kernel, act=act),
        out_shape=jax.ShapeDtypeStruct((m, n), out_dtype),
        grid=(m // tm, n // tn),
        in_specs=[pl.BlockSpec((tm, k), lambda i, j: (i, 0)), pl.BlockSpec((k, tn), lambda i, j: (0, j))],
        out_specs=pl.BlockSpec((tm, tn), lambda i, j: (i, j)),
        compiler_params=_params("parallel", "arbitrary"),
        name=name,
    )(a, b)


def _glu_kernel(a_ref, b1_ref, b2_ref, o_ref):
    a = a_ref[...]
    lin = jnp.dot(a, b1_ref[...], preferred_element_type=f32)
    gate = jnp.dot(a, b2_ref[...], preferred_element_type=f32)
    o_ref[...] = lin * _sigmoid(gate)


def _matmul_glu(a, b):
    m, k = a.shape
    n = b.shape[1] // 2
    tm, tn = _tile(m, 512), _tile(n, 512)
    nj = n // tn
    return pl.pallas_call(
        _glu_kernel,
        out_shape=jax.ShapeDtypeStruct((m, n), f32),
        grid=(m // tm, nj),
        in_specs=[pl.BlockSpec((tm, k), lambda i, j: (i, 0)),
                  pl.BlockSpec((k, tn), lambda i, j: (0, j)),
                  pl.BlockSpec((k, tn), lambda i, j: (0, j + nj))],
        out_specs=pl.BlockSpec((tm, tn), lambda i, j: (i, j)),
        compiler_params=_params("parallel", "arbitrary"),
        name="mm_glu",
    )(a, b, b)


def _merge_kernel(a1_ref, a2_ref, b1_ref, b2_ref, ga_ref, gb_ref, o_ref):
    ya = jnp.dot(a1_ref[...], b1_ref[...], preferred_element_type=f32)
    yb = jnp.dot(a2_ref[...], b2_ref[...], preferred_element_type=f32)
    o_ref[...] = (ga_ref[...] * ya + gb_ref[...] * yb).astype(o_ref.dtype)


def _matmul_merge(a1, a2, b1, b2, gates):
    m, k = a1.shape
    n = b1.shape[1]
    tm, tn = _tile(m, 512), _tile(n, 512)
    nj = n // tn
    return pl.pallas_call(
        _merge_kernel,
        out_shape=jax.ShapeDtypeStruct((m, n), bf16),
        grid=(m // tm, nj),
        in_specs=[pl.BlockSpec((tm, k), lambda i, j: (i, 0)),
                  pl.BlockSpec((tm, k), lambda i, j: (i, 0)),
                  pl.BlockSpec((k, tn), lambda i, j: (0, j)),
                  pl.BlockSpec((k, tn), lambda i, j: (0, j)),
                  pl.BlockSpec((tm, tn), lambda i, j: (i, j)),
                  pl.BlockSpec((tm, tn), lambda i, j: (i, j + nj))],
        out_specs=pl.BlockSpec((tm, tn), lambda i, j: (i, j)),
        compiler_params=_params("parallel", "arbitrary"),
        name="mm_merge",
    )(a1, a2, b1, b2, gates, gates)


def _out_proj_kernel(a_ref, b_ref, x_ref, g1_ref, g2_ref, xo_ref, ho_ref, acc_ref):
    kstep = pl.program_id(1)

    @pl.when(kstep == 0)
    def _():
        acc_ref[...] = jnp.zeros_like(acc_ref)

    acc_ref[...] += jnp.dot(a_ref[...], b_ref[...], preferred_element_type=f32)

    @pl.when(kstep == pl.num_programs(1) - 1)
    def _():
        x1 = x_ref[...] + _rms(acc_ref[...], g1_ref[...])
        xo_ref[...] = x1
        ho_ref[...] = _rms(x1, g2_ref[...]).astype(ho_ref.dtype)


def _out_proj(a, b, x, g_post, g_next):
    m, k = a.shape
    n = b.shape[1]
    tm, tk = _tile(m, 256), _tile(k, 512)
    return pl.pallas_call(
        _out_proj_kernel,
        out_shape=(jax.ShapeDtypeStruct((m, n), f32), jax.ShapeDtypeStruct((m, n), bf16)),
        grid=(m // tm, k // tk),
        in_specs=[pl.BlockSpec((tm, tk), lambda i, kk: (i, kk)),
                  pl.BlockSpec((tk, n), lambda i, kk: (kk, 0)),
                  pl.BlockSpec((tm, n), lambda i, kk: (i, 0)),
                  pl.BlockSpec((1, n), lambda i, kk: (0, 0)),
                  pl.BlockSpec((1, n), lambda i, kk: (0, 0))],
        out_specs=(pl.BlockSpec((tm, n), lambda i, kk: (i, 0)),
                   pl.BlockSpec((tm, n), lambda i, kk: (i, 0))),
        scratch_shapes=[pltpu.VMEM((tm, n), f32)],
        compiler_params=_params("parallel", "arbitrary"),
        name="out_proj",
    )(a, b, x, g_post, g_next)


def _ffn_kernel(h_ref, wu_ref, wd_ref, x_ref, g1_ref, g2_ref, xo_ref, ho_ref, acc_ref):
    fstep = pl.program_id(1)

    @pl.when(fstep == 0)
    def _():
        acc_ref[...] = jnp.zeros_like(acc_ref)

    up = jnp.dot(h_ref[...], wu_ref[...], preferred_element_type=f32)
    act = jnp.square(jnp.maximum(up, 0.0)).astype(bf16)
    acc_ref[...] += jnp.dot(act, wd_ref[...], preferred_element_type=f32)

    @pl.when(fstep == pl.num_programs(1) - 1)
    def _():
        x2 = x_ref[...] + _rms(acc_ref[...], g1_ref[...])
        xo_ref[...] = x2
        ho_ref[...] = _rms(x2, g2_ref[...]).astype(ho_ref.dtype)


def _ffn(h, w_up, w_down, x, g_post, g_next):
    m, d = h.shape
    dff = w_up.shape[1]
    tm, tf = _tile(m, 256), _tile(dff, 512)
    return pl.pallas_call(
        _ffn_kernel,
        out_shape=(jax.ShapeDtypeStruct((m, d), f32), jax.ShapeDtypeStruct((m, d), bf16)),
        grid=(m // tm, dff // tf),
        in_specs=[pl.BlockSpec((tm, d), lambda i, j: (i, 0)),
                  pl.BlockSpec((d, tf), lambda i, j: (0, j)),
                  pl.BlockSpec((tf, d), lambda i, j: (j, 0)),
                  pl.BlockSpec((tm, d), lambda i, j: (i, 0)),
                  pl.BlockSpec((1, d), lambda i, j: (0, 0)),
                  pl.BlockSpec((1, d), lambda i, j: (0, 0))],
        out_specs=(pl.BlockSpec((tm, d), lambda i, j: (i, 0)),
                   pl.BlockSpec((tm, d), lambda i, j: (i, 0))),
        scratch_shapes=[pltpu.VMEM((tm, d), f32)],
        compiler_params=_params("parallel", "arbitrary"),
        name="ffn",
    )(h, w_up, w_down, x, g_post, g_next)


def _ple_kernel(h_ref, wg_ref, p_ref, wp_ref, x_ref, g_ref, o_ref, acc_ref):
    kstep = pl.program_id(1)

    @pl.when(kstep == 0)
    def _():
        acc_ref[...] = jnp.zeros_like(acc_ref)

    acc_ref[...] += jnp.dot(h_ref[...], wg_ref[...], preferred_element_type=f32)

    @pl.when(kstep == pl.num_programs(1) - 1)
    def _():
        pe = jnp.dot(p_ref[...].astype(bf16), wp_ref[...], preferred_element_type=f32)
        o_ref[...] = x_ref[...] + _rms(pe * _sigmoid(acc_ref[...]), g_ref[...])


def _ple(h, w_gate, p, w_ple, x, g_post):
    m, k = h.shape
    n = w_gate.shape[1]
    q = p.shape[1]
    tm, tk = _tile(m, 256), _tile(k, 512)
    return pl.pallas_call(
        _ple_kernel,
        out_shape=jax.ShapeDtypeStruct((m, n), f32),
        grid=(m // tm, k // tk),
        in_specs=[pl.BlockSpec((tm, tk), lambda i, kk: (i, kk)),
                  pl.BlockSpec((tk, n), lambda i, kk: (kk, 0)),
                  pl.BlockSpec((tm, q), lambda i, kk: (i, 0)),
                  pl.BlockSpec((q, n), lambda i, kk: (0, 0)),
                  pl.BlockSpec((tm, n), lambda i, kk: (i, 0)),
                  pl.BlockSpec((1, n), lambda i, kk: (0, 0))],
        out_specs=pl.BlockSpec((tm, n), lambda i, kk: (i, 0)),
        scratch_shapes=[pltpu.VMEM((tm, n), f32)],
        compiler_params=_params("parallel", "arbitrary"),
        name="ple",
    )(h, w_gate, p, w_ple, x, g_post)


def _shift(x, bnd, mu, seg_len):
    nb, _, w = x.shape
    rows = nb * seg_len
    x2 = x.reshape(rows, w)
    prev = pltpu.roll(x2, 1, 0)
    first = (lax.broadcasted_iota(jnp.int32, (rows, w), 0) % seg_len) == 0
    bnd_rows = jnp.broadcast_to(bnd, (nb, seg_len, w)).reshape(rows, w)
    prev = jnp.where(first, bnd_rows, prev)
    return x2 + (prev - x2) * mu


def _prep_kernel(pr_ref, pk_ref, pv_ref, br_ref, bk_ref, bv_ref, mur_ref, muk_ref, muv_ref,
                 pl_ref, bl_ref, mul_ref, w0_ref, a0_ref, kkw_ref, kaw_ref,
                 w2_ref, a2_ref, g2_ref, e_ref,
                 r_ref, lw_ref, k_ref, v_ref, kk_ref, a_ref, g_ref,
                 tw_ref, al_ref, sg_ref, *, seg_len, lora):
    dl, al_n, gl_n = lora

    @pl.when(pl.program_id(1) == 0)
    def _():
        xl = _shift(pl_ref[...], bl_ref[...], mul_ref[...], seg_len)
        tw_ref[...] = jnp.tanh(xl[:, :dl]).astype(bf16)
        al_ref[...] = xl[:, dl:dl + al_n].astype(bf16)
        sg_ref[...] = _sigmoid(xl[:, dl + al_n:dl + al_n + gl_n]).astype(bf16)

    shape = r_ref.shape
    r = _shift(pr_ref[...], br_ref[...], mur_ref[...], seg_len)
    k = _shift(pk_ref[...], bk_ref[...], muk_ref[...], seg_len)
    v = _shift(pv_ref[...], bv_ref[...], muv_ref[...], seg_len)

    dec = w0_ref[...] + jnp.dot(tw_ref[...], w2_ref[...], preferred_element_type=f32)
    z = -dec
    softplus = jnp.maximum(z, 0.0) + jnp.log(1.0 + jnp.exp(-jnp.abs(z)))
    lw = -jnp.exp(-softplus - 0.5)
    a = _sigmoid(a0_ref[...] + jnp.dot(al_ref[...], a2_ref[...], preferred_element_type=f32))
    g = jnp.dot(sg_ref[...], g2_ref[...], preferred_element_type=f32)

    kk = k * kkw_ref[...]
    ssq = _dot_exact_rhs(kk * kk, e_ref[...])
    kk = kk / jnp.maximum(jnp.sqrt(ssq), KK_EPS)
    k = k * (1.0 + (a - 1.0) * kaw_ref[...])

    r_ref[...] = r.reshape(shape)
    lw_ref[...] = lw.reshape(shape)
    k_ref[...] = k.reshape(shape)
    v_ref[...] = v.reshape(shape)
    kk_ref[...] = kk.reshape(shape)
    a_ref[...] = a.reshape(shape)
    g_ref[...] = g.reshape(shape)


def _head_indicator(width):
    h = jnp.arange(width) // HEAD_SIZE
    return (h[:, None] == h[None, :]).astype(bf16)


def _prep(pa_rkv, bnd_rkv, pa_lora, bnd_lora, mu_rkv, mu_lora, w0, a0, k_k, k_a, w2, a2, g2,
          seg_per_tile):
    nseg, seg_len, c3 = pa_rkv.shape
    c = c3 // 3
    wl = pa_lora.shape[2]
    lora = (w2.shape[0], a2.shape[0], g2.shape[0])
    nb = seg_per_tile
    tc = _tile(c, 512)
    nj = c // tc
    rows = nb * seg_len
    tok = lambda off: pl.BlockSpec((nb, seg_len, tc), lambda i, j: (i, 0, j + off))
    bnd = lambda off: pl.BlockSpec((nb, 1, tc), lambda i, j: (i, 0, j + off))
    vec = lambda off: pl.BlockSpec((1, tc), lambda i, j: (0, j + off))
    low = lambda n: pl.BlockSpec((n, tc), lambda i, j: (0, j))
    out = jax.ShapeDtypeStruct((nseg, seg_len, c), f32)
    return pl.pallas_call(
        functools.partial(_prep_kernel, seg_len=seg_len, lora=lora),
        out_shape=(out,) * 7,
        grid=(nseg // nb, nj),
        in_specs=[tok(0), tok(nj), tok(2 * nj), bnd(0), bnd(nj), bnd(2 * nj), vec(0), vec(nj), vec(2 * nj),
                  pl.BlockSpec((nb, seg_len, wl), lambda i, j: (i, 0, 0)),
                  pl.BlockSpec((nb, 1, wl), lambda i, j: (i, 0, 0)),
                  pl.BlockSpec((1, wl), lambda i, j: (0, 0)),
                  vec(0), vec(0), vec(0), vec(0),
                  low(lora[0]), low(lora[1]), low(lora[2]),
                  pl.BlockSpec((tc, tc), lambda i, j: (0, 0))],
        out_specs=(pl.BlockSpec((nb, seg_len, tc), lambda i, j: (i, 0, j)),) * 7,
        scratch_shapes=[pltpu.VMEM((rows, lora[0]), bf16), pltpu.VMEM((rows, lora[1]), bf16),
                        pltpu.VMEM((rows, lora[2]), bf16)],
        compiler_params=_params("parallel", "arbitrary"),
        name="rwkv_prep",
    )(pa_rkv, pa_rkv, pa_rkv, bnd_rkv, bnd_rkv, bnd_rkv, mu_rkv, mu_rkv, mu_rkv,
      pa_lora, bnd_lora, mu_lora, w0, a0, k_k, k_a, w2, a2, g2, _head_indicator(tc))


def _wkv_chunk(r, lw, k, v, kk, a, s0_list, tri, strict, incl, eye, chunk, dot):
    n = HEAD_SIZE
    lw_hi, lw_lo = _split(lw)
    lw_lo2 = (lw - lw_hi.astype(f32) - lw_lo.astype(f32)).astype(bf16)
    cs = lambda x: lax.dot_general(tri, x, _NN, preferred_element_type=f32)
    cl = cs(lw_hi) + (cs(lw_lo) + cs(lw_lo2))
    cl_last = cl[chunk - 1:chunk, :]
    at = -kk * jnp.exp(cl - lw)
    rt = r * jnp.exp(cl)
    einv = jnp.exp(-cl)
    etail = jnp.exp(cl_last - cl)
    b = kk * a
    bt = b * einv
    kt = k * einv
    bh = b * etail
    kh = k * etail
    w_tot = jnp.exp(cl_last)

    ys, s_new = [], []
    for h, s0 in enumerate(s0_list):
        sl = slice(h * n, (h + 1) * n)
        at_h, rt_h, bt_h, kt_h, v_h = at[:, sl], rt[:, sl], bt[:, sl], kt[:, sl], v[:, sl]
        a_ab = jnp.where(strict, dot(at_h, bt_h, _NT), 0.0)
        a_ak = jnp.where(strict, dot(at_h, kt_h, _NT), 0.0)
        a_rb = jnp.where(incl, dot(rt_h, bt_h, _NT), 0.0)
        a_rk = jnp.where(incl, dot(rt_h, kt_h, _NT), 0.0)
        inv = eye + a_ab
        apow = a_ab
        steps = chunk.bit_length() - 2
        for _ in range(steps):
            apow = dot(apow, apow)
            inv = inv + dot(inv, apow)
        sa = dot(inv, dot(a_ak, v_h) + dot(at_h, s0, _NT))
        ys.append(dot(rt_h, s0, _NT) + dot(a_rb, sa) + dot(a_rk, v_h))
        s_new.append(s0 * w_tot[:, sl] + dot(sa, bh[:, sl], _TN) + dot(v_h, kh[:, sl], _TN))
    return ys, s_new


def _wkv_kernel(r_ref, lw_ref, k_ref, v_ref, kk_ref, a_ref, g_ref, rk_ref, lng_ref, lnb_ref, s0_ref,
                o_ref, st_ref, state_ref, *, chunk, heads, precise):
    n = HEAD_SIZE
    tstep = pl.program_id(2)
    dot = _dot3 if precise else _dot

    @pl.when(tstep == 0)
    def _():
        state_ref[...] = s0_ref[0]

    row = lax.broadcasted_iota(jnp.int32, (chunk, chunk), 0)
    col = lax.broadcasted_iota(jnp.int32, (chunk, chunk), 1)
    incl = row >= col
    strict = row > col
    tri = incl.astype(bf16)
    eye = (row == col).astype(f32)

    def body(ci, carry):
        rows = pl.ds(pl.multiple_of(ci * chunk, chunk), chunk)
        r, lw, k, v = r_ref[0, rows, :], lw_ref[0, rows, :], k_ref[0, rows, :], v_ref[0, rows, :]
        kk, a, g = kk_ref[0, rows, :], a_ref[0, rows, :], g_ref[0, rows, :]
        s0_list = [state_ref[h] for h in range(heads)]
        ys, s_new = _wkv_chunk(r, lw, k, v, kk, a, s0_list, tri, strict, incl, eye, chunk, dot)
        for h in range(heads):
            state_ref[h] = s_new[h]
        bonus_w = r * k * rk_ref[...]
        outs = []
        for h, y in enumerate(ys):
            sl = slice(h * n, (h + 1) * n)
            mu = jnp.mean(y, axis=-1, keepdims=True)
            d = y - mu
            var = jnp.mean(d * d, axis=-1, keepdims=True)
            bonus = jnp.sum(bonus_w[:, sl], axis=-1, keepdims=True) * v[:, sl]
            outs.append(d * lax.rsqrt(var + GN_EPS) * lng_ref[:, sl] + lnb_ref[:, sl] + bonus)
        o_ref[0, rows, :] = (jnp.concatenate(outs, axis=-1) * g).astype(o_ref.dtype)
        return carry

    lax.fori_loop(0, r_ref.shape[1] // chunk, body, 0)

    @pl.when(tstep == pl.num_programs(2) - 1)
    def _():
        st_ref[0] = state_ref[...]


def _wkv(r, lw, k, v, kk, a, g, r_k, lnx_g, lnx_b, s0, chunk, rows_per_step, heads=4, precise=True):
    bsz, t, c = r.shape
    n = HEAD_SIZE
    hn = heads * n
    lt = rows_per_step
    tok = pl.BlockSpec((1, lt, hn), lambda b, gi, ti: (b, ti, gi))
    vec = pl.BlockSpec((1, hn), lambda b, gi, ti: (0, gi))
    st = pl.BlockSpec((1, heads, n, n), lambda b, gi, ti: (b, gi, 0, 0))
    return pl.pallas_call(
        functools.partial(_wkv_kernel, chunk=chunk, heads=heads, precise=precise),
        out_shape=(jax.ShapeDtypeStruct((bsz, t, c), bf16), jax.ShapeDtypeStruct(s0.shape, f32)),
        grid=(bsz, c // hn, t // lt),
        in_specs=[tok] * 7 + [vec] * 3 + [st],
        out_specs=(tok, st),
        scratch_shapes=[pltpu.VMEM((heads, n, n), f32)],
        compiler_params=_params("parallel", "parallel", "arbitrary"),
        name="wkv",
    )(r, lw, k, v, kk, a, g, r_k, lnx_g, lnx_b, s0)


def _conv_kernel(u_ref, halo_ref, st_ref, w_ref, b_ref, o_ref, ext_ref, *, taps):
    nb, tt, _ = u_ref.shape
    first = pl.program_id(1) == 0

    @pl.when(first)
    def _():
        ext_ref[:, 0:CONV_HALO, :] = st_ref[...]

    @pl.when(jnp.logical_not(first))
    def _():
        ext_ref[:, 0:CONV_HALO, :] = halo_ref[...]

    ext_ref[:, CONV_HALO:CONV_HALO + tt, :] = u_ref[...]
    lead = CONV_HALO - (taps - 1)
    acc = jnp.zeros(u_ref.shape, f32) + b_ref[...]
    for tap in range(taps):
        acc = acc + w_ref[tap:tap + 1, :] * ext_ref[:, lead + tap:lead + tap + tt, :]
    o_ref[...] = acc


def _conv(u, hist, dw_w, dw_b, seq_per_tile, rows_per_tile):
    bsz, t, c = u.shape
    nb, tt = seq_per_tile, rows_per_tile
    tc = _tile(c, 256)
    hpt = tt // CONV_HALO
    taps = dw_w.shape[0]
    return pl.pallas_call(
        functools.partial(_conv_kernel, taps=taps),
        out_shape=jax.ShapeDtypeStruct((bsz, t, c), f32),
        grid=(bsz // nb, t // tt, c // tc),
        in_specs=[pl.BlockSpec((nb, tt, tc), lambda b, i, j: (b, i, j)),
                  pl.BlockSpec((nb, CONV_HALO, tc), lambda b, i, j: (b, jnp.maximum(i * hpt - 1, 0), j)),
                  pl.BlockSpec((nb, CONV_HALO, tc), lambda b, i, j: (b, 0, j)),
                  pl.BlockSpec((taps, tc), lambda b, i, j: (0, j)),
                  pl.BlockSpec((1, tc), lambda b, i, j: (0, j))],
        out_specs=pl.BlockSpec((nb, tt, tc), lambda b, i, j: (b, i, j)),
        scratch_shapes=[pltpu.VMEM((nb, CONV_HALO + tt, tc), f32)],
        compiler_params=_params("parallel", "arbitrary", "arbitrary"),
        name="dwconv",
    )(u, u, hist, dw_w, dw_b)


def _ln_silu_kernel(x_ref, g_ref, b_ref, o_ref):
    x = x_ref[...]
    mu = jnp.mean(x, axis=-1, keepdims=True)
    d = x - mu
    var = jnp.mean(d * d, axis=-1, keepdims=True)
    y = d * lax.rsqrt(var + LN_EPS) * g_ref[...] + b_ref[...]
    o_ref[...] = (y * _sigmoid(y)).astype(o_ref.dtype)


def _ln_silu(x, g, b):
    m, d = x.shape
    tm = _tile(m, 512)
    return pl.pallas_call(
        _ln_silu_kernel,
        out_shape=jax.ShapeDtypeStruct((m, d), bf16),
        grid=(m // tm,),
        in_specs=[pl.BlockSpec((tm, d), lambda i: (i, 0)), pl.BlockSpec((1, d), lambda i: (0, 0)),
                  pl.BlockSpec((1, d), lambda i: (0, 0))],
        out_specs=pl.BlockSpec((tm, d), lambda i: (i, 0)),
        compiler_params=_params("parallel"),
        name="ln_silu",
    )(x, g, b)


def _layer(x, p, s_wkv, s_shift, s_conv, w, *, chunk, seg_len):
    bsz, t, d = x.shape
    m = bsz * t
    c = w["w_o_a"].shape[0]
    c3 = 3 * c
    n_shift = s_shift.shape[1]
    n_lora = n_shift - c3
    x2 = x.reshape(m, d)

    h = _rms_cast(x2, w["g_pre_mix"])
    pa_rkv = _matmul(h, w["w_in_rkv"], name="mm_rkv")
    pa_lora = _matmul(h, w["w_in_lora"], tn_pref=w["w_in_lora"].shape[1], name="mm_lora")
    u = _matmul_glu(h, w["w_in_conv"])
    gates = _matmul(h, w["w_in_gate"], act="sigmoid", name="mm_gate")

    spb = t // seg_len
    nseg = bsz * spb
    lw_pad = pa_lora.shape[1]

    def boundaries(pa, carried):
        width = pa.shape[1]
        tails = pa.reshape(bsz, spb, seg_len, width)[:, :-1, seg_len - 1, :]
        return jnp.concatenate([carried[:, None, :], tails], axis=1).reshape(nseg, 1, width)

    bnd_rkv = boundaries(pa_rkv, s_shift[:, :c3])
    bnd_lora = boundaries(pa_lora, jnp.pad(s_shift[:, c3:], ((0, 0), (0, lw_pad - n_lora))))
    seg_per_tile = _tile(nseg, max(1, 256 // seg_len))
    prep = _prep(pa_rkv.reshape(nseg, seg_len, c3), bnd_rkv, pa_lora.reshape(nseg, seg_len, lw_pad), bnd_lora,
                 w["mu_rkv"], w["mu_lora"], w["w0"], w["a0"], w["k_k"], w["k_a"], w["w2"], w["a2"], w["g2"],
                 seg_per_tile)
    r, lw, k, v, kk, a, g = (z.reshape(bsz, t, c) for z in prep)
    ya_in, s_new = _wkv(r, lw, k, v, kk, a, g, w["r_k"], w["lnx_g"], w["lnx_b"], s_wkv,
                        chunk=chunk, rows_per_step=_tile(t, 256))

    last = (jnp.arange(bsz) + 1) * t - 1
    shift_new = jnp.concatenate([pa_rkv[last], pa_lora[last, :n_lora]], axis=1)

    u3 = u.reshape(bsz, t, c)
    hist = jnp.pad(s_conv, ((0, 0), (CONV_HALO - s_conv.shape[1], 0), (0, 0)))
    conv_rows = _tile(t, 256)
    cv = _conv(u3, hist, w["dw_w"], w["dw_b"], _tile(bsz, max(1, 256 // conv_rows)), conv_rows)
    cv = _ln_silu(cv.reshape(m, c), w["lnc_g"], w["lnc_b"])
    keep = s_conv.shape[1]
    conv_new = jnp.concatenate([s_conv, u3], axis=1)[:, -keep:] if t < keep else u3[:, t - keep:]

    merged = _matmul_merge(ya_in.reshape(m, c), cv, w["w_o_a"], w["w_o_b"], gates)
    x1, hf = _out_proj(merged, w["w_out"], x2, w["g_post_mix"], w["g_pre_ffn"])
    x2_, hp = _ffn(hf, w["w_up"], w["w_down"], x1, w["g_post_ffn"], w["g_pre_ple"])
    y = _ple(hp, w["w_ple_gate"], p.reshape(m, p.shape[-1]), w["w_ple"], x2_, w["g_post_ple"])
    return y.reshape(bsz, t, d), s_new, shift_new, conv_new


def _layer_weights(i, c, g_pre_mix, w_in, mu_shift, w0, w2, a0, a2, g2, k_k, k_a, r_k, lnx_g, lnx_b, w_o_a,
                   dw_w, dw_b, lnc_g, lnc_b, w_o_b, w_out, g_post_mix, g_pre_ffn, w_up, w_down, g_post_ffn,
                   g_pre_ple, w_ple_gate, w_ple, g_post_ple):
    c3 = 3 * c
    n_lora = w2.shape[1] + a2.shape[1] + g2.shape[1]
    n_shift = c3 + n_lora
    lw_pad = -(-n_lora // 128) * 128
    row = lambda z: z[i].reshape(1, -1)
    win = w_in[i]
    return {
        "g_pre_mix": row(g_pre_mix),
        "w_in_rkv": win[:, :c3].astype(bf16),
        "w_in_lora": jnp.pad(win[:, c3:n_shift], ((0, 0), (0, lw_pad - n_lora))).astype(bf16),
        "w_in_conv": win[:, n_shift:n_shift + 2 * c].astype(bf16),
        "w_in_gate": win[:, n_shift + 2 * c:].astype(bf16),
        "mu_rkv": mu_shift[i, :c3].reshape(1, -1),
        "mu_lora": jnp.pad(mu_shift[i, c3:], (0, lw_pad - n_lora)).reshape(1, -1),
        "w0": row(w0), "a0": row(a0), "k_k": row(k_k), "k_a": row(k_a),
        "w2": w2[i].astype(bf16), "a2": a2[i].astype(bf16), "g2": g2[i].astype(bf16),
        "r_k": row(r_k), "lnx_g": row(lnx_g), "lnx_b": row(lnx_b),
        "w_o_a": w_o_a[i].astype(bf16), "w_o_b": w_o_b[i].astype(bf16), "w_out": w_out[i].astype(bf16),
        "dw_w": dw_w[i], "dw_b": row(dw_b), "lnc_g": row(lnc_g), "lnc_b": row(lnc_b),
        "g_post_mix": row(g_post_mix), "g_pre_ffn": row(g_pre_ffn),
        "w_up": w_up[i].astype(bf16), "w_down": w_down[i].astype(bf16),
        "g_post_ffn": row(g_post_ffn), "g_pre_ple": row(g_pre_ple),
        "w_ple_gate": w_ple_gate[i].astype(bf16), "w_ple": w_ple[i].astype(bf16),
        "g_post_ple": row(g_post_ple),
    }


def kernel(x_prompt, x_sample, p_prompt, p_sample, state_wkv, state_shift, state_conv, g_pre_mix, w_in, mu_shift, w0, w2, a0, a2, g2, k_k, k_a, r_k, lnx_g, lnx_b, w_o_a, dw_w, dw_b, lnc_g, lnc_b, w_o_b, w_out, g_post_mix, g_pre_ffn, w_up, w_down, g_post_ffn, g_pre_ple, w_ple_gate, w_ple, g_post_ple):
    depth = w_in.shape[0]
    c = w_o_a.shape[1]
    heads = c // HEAD_SIZE
    bp, tp, _ = x_prompt.shape
    ts = x_sample.shape[1]
    xp, xs = x_prompt, x_sample
    outs = [[] for _ in range(6)]
    for i in range(depth):
        w = _layer_weights(i, c, g_pre_mix, w_in, mu_shift, w0, w2, a0, a2, g2, k_k, k_a, r_k, lnx_g, lnx_b,
                           w_o_a, dw_w, dw_b, lnc_g, lnc_b, w_o_b, w_out, g_post_mix, g_pre_ffn, w_up, w_down,
                           g_post_ffn, g_pre_ple, w_ple_gate, w_ple, g_post_ple)
        xp, s1, s2, s3 = _layer(xp, p_prompt[i],
                                jnp.zeros((bp, heads, HEAD_SIZE, HEAD_SIZE), f32),
                                jnp.zeros((bp, state_shift.shape[2]), f32),
                                jnp.zeros((bp,) + state_conv.shape[2:], f32), w,
                                chunk=min(64, tp), seg_len=_tile(tp, 256))
        xs, t1, t2, t3 = _layer(xs, p_sample[i], state_wkv[i], state_shift[i], state_conv[i], w,
                                chunk=min(64, ts), seg_len=ts)
        for lst, val in zip(outs, (s1, s2, s3, t1, t2, t3)):
            lst.append(val)
    return (xp, xs) + tuple(jnp.stack(o) for o in outs)
```

```python
import functools

import jax
import jax.numpy as jnp
from jax import lax
from jax.experimental import pallas as pl
from jax.experimental.pallas import tpu as pltpu

f32 = jnp.float32
bf16 = jnp.bfloat16

HEAD_SIZE = 64
CONV_K = 31
CONV_HALO = 32
RMS_EPS = 1e-6
LN_EPS = 1e-5
GN_EPS = 64e-5
KK_EPS = 1e-12
VMEM_LIMIT = 56 * 1024 * 1024

_NN = (((1,), (0,)), ((), ()))
_NT = (((1,), (1,)), ((), ()))
_TN = (((0,), (0,)), ((), ()))


def _params(*sem):
    return pltpu.CompilerParams(dimension_semantics=sem, vmem_limit_bytes=VMEM_LIMIT)


def _tile(n, pref):
    t = min(n, pref)
    while n % t:
        t -= 1
    return t


def _dot(a, b, dims=_NN):
    return lax.dot_general(a.astype(bf16), b.astype(bf16), dims, preferred_element_type=f32)


def _split(a):
    hi = a.astype(bf16)
    lo = (a - hi.astype(f32)).astype(bf16)
    return hi, lo


def _dot3(a, b, dims=_NN):
    a_hi, a_lo = _split(a)
    b_hi, b_lo = _split(b)
    d = functools.partial(lax.dot_general, dimension_numbers=dims, preferred_element_type=f32)
    return d(a_hi, b_hi) + (d(a_hi, b_lo) + d(a_lo, b_hi))


def _dot_exact_rhs(a, b_bf16, dims=_NN):
    a_hi, a_lo = _split(a)
    d = functools.partial(lax.dot_general, dimension_numbers=dims, preferred_element_type=f32)
    return d(a_hi, b_bf16) + d(a_lo, b_bf16)


def _sigmoid(x):
    return 1.0 / (1.0 + jnp.exp(-x))


def _rms(x, g):
    return x * lax.rsqrt(jnp.mean(x * x, axis=-1, keepdims=True) + RMS_EPS) * g


def _rms_cast_kernel(x_ref, g_ref, o_ref):
    o_ref[...] = _rms(x_ref[...], g_ref[...]).astype(o_ref.dtype)


def _rms_cast(x, g):
    m, d = x.shape
    tm = _tile(m, 512)
    return pl.pallas_call(
        _rms_cast_kernel,
        out_shape=jax.ShapeDtypeStruct((m, d), bf16),
        grid=(m // tm,),
        in_specs=[pl.BlockSpec((tm, d), lambda i: (i, 0)), pl.BlockSpec((1, d), lambda i: (0, 0))],
        out_specs=pl.BlockSpec((tm, d), lambda i: (i, 0)),
        compiler_params=_params("parallel"),
        name="rms_cast",
    )(x, g)


def _mm_kernel(a_ref, b_ref, o_ref, *, act):
    acc = jnp.dot(a_ref[...], b_ref[...], preferred_element_type=f32)
    if act == "sigmoid":
        acc = _sigmoid(acc)
    o_ref[...] = acc.astype(o_ref.dtype)


def _matmul(a, b, act=None, out_dtype=f32, tm_pref=512, tn_pref=1024, name="mm"):
    m, k = a.shape
    n = b.shape[1]
    tm, tn = _tile(m, tm_pref), _tile(n, tn_pref)
    return pl.pallas_call(
        functools.partial(_mm_kernel, act=act),
        out_shape=jax.ShapeDtypeStruct((m, n), out_dtype),
        grid=(m // tm, n // tn),
        in_specs=[pl.BlockSpec((tm, k), lambda i, j: (i, 0)), pl.BlockSpec((k, tn), lambda i, j: (0, j))],
        out_specs=pl.BlockSpec((tm, tn), lambda i, j: (i, j)),
        compiler_params=_params("parallel", "arbitrary"),
        name=name,
    )(a, b)


def _glu_kernel(a_ref, b1_ref, b2_ref, o_ref):
    a = a_ref[...]
    lin = jnp.dot(a, b1_ref[...], preferred_element_type=f32)
    gate = jnp.dot(a, b2_ref[...], preferred_element_type=f32)
    o_ref[...] = lin * _sigmoid(gate)


def _matmul_glu(a, b):
    m, k = a.shape
    n = b.shape[1] // 2
    tm, tn = _tile(m, 512), _tile(n, 512)
    nj = n // tn
    return pl.pallas_call(
        _glu_kernel,
        out_shape=jax.ShapeDtypeStruct((m, n), f32),
        grid=(m // tm, nj),
        in_specs=[pl.BlockSpec((tm, k), lambda i, j: (i, 0)),
                  pl.BlockSpec((k, tn), lambda i, j: (0, j)),
                  pl.BlockSpec((k, tn), lambda i, j: (0, j + nj))],
        out_specs=pl.BlockSpec((tm, tn), lambda i, j: (i, j)),
        compiler_params=_params("parallel", "arbitrary"),
        name="mm_glu",
    )(a, b, b)


def _merge_kernel(a1_ref, a2_ref, b1_ref, b2_ref, ga_ref, gb_ref, o_ref):
    ya = jnp.dot(a1_ref[...], b1_ref[...], preferred_element_type=f32)
    yb = jnp.dot(a2_ref[...], b2_ref[...], preferred_element_type=f32)
    o_ref[...] = (ga_ref[...] * ya + gb_ref[...] * yb).astype(o_ref.dtype)


def _matmul_merge(a1, a2, b1, b2, gates):
    m, k = a1.shape
    n = b1.shape[1]
    tm, tn = _tile(m, 512), _tile(n, 512)
    nj = n // tn
    return pl.pallas_call(
        _merge_kernel,
        out_shape=jax.ShapeDtypeStruct((m, n), bf16),
        grid=(m // tm, nj),
        in_specs=[pl.BlockSpec((tm, k), lambda i, j: (i, 0)),
                  pl.BlockSpec((tm, k), lambda i, j: (i, 0)),
                  pl.BlockSpec((k, tn), lambda i, j: (0, j)),
                  pl.BlockSpec((k, tn), lambda i, j: (0, j)),
                  pl.BlockSpec((tm, tn), lambda i, j: (i, j)),
                  pl.BlockSpec((tm, tn), lambda i, j: (i, j + nj))],
        out_specs=pl.BlockSpec((tm, tn), lambda i, j: (i, j)),
        compiler_params=_params("parallel", "arbitrary"),
        name="mm_merge",
    )(a1, a2, b1, b2, gates, gates)


def _out_proj_kernel(a_ref, b_ref, x_ref, g1_ref, g2_ref, xo_ref, ho_ref, acc_ref):
    kstep = pl.program_id(1)

    @pl.when(kstep == 0)
    def _():
        acc_ref[...] = jnp.zeros_like(acc_ref)

    acc_ref[...] += jnp.dot(a_ref[...], b_ref[...], preferred_element_type=f32)

    @pl.when(kstep == pl.num_programs(1) - 1)
    def _():
        x1 = x_ref[...] + _rms(acc_ref[...], g1_ref[...])
        xo_ref[...] = x1
        ho_ref[...] = _rms(x1, g2_ref[...]).astype(ho_ref.dtype)


def _out_proj(a, b, x, g_post, g_next):
    m, k = a.shape
    n = b.shape[1]
    tm, tk = _tile(m, 256), _tile(k, 512)
    return pl.pallas_call(
        _out_proj_kernel,
        out_shape=(jax.ShapeDtypeStruct((m, n), f32), jax.ShapeDtypeStruct((m, n), bf16)),
        grid=(m // tm, k // tk),
        in_specs=[pl.BlockSpec((tm, tk), lambda i, kk: (i, kk)),
                  pl.BlockSpec((tk, n), lambda i, kk: (kk, 0)),
                  pl.BlockSpec((tm, n), lambda i, kk: (i, 0)),
                  pl.BlockSpec((1, n), lambda i, kk: (0, 0)),
                  pl.BlockSpec((1, n), lambda i, kk: (0, 0))],
        out_specs=(pl.BlockSpec((tm, n), lambda i, kk: (i, 0)),
                   pl.BlockSpec((tm, n), lambda i, kk: (i, 0))),
        scratch_shapes=[pltpu.VMEM((tm, n), f32)],
        compiler_params=_params("parallel", "arbitrary"),
        name="out_proj",
    )(a, b, x, g_post, g_next)


def _ffn_kernel(h_ref, wu_ref, wd_ref, x_ref, g1_ref, g2_ref, xo_ref, ho_ref, acc_ref):
    fstep = pl.program_id(1)

    @pl.when(fstep == 0)
    def _():
        acc_ref[...] = jnp.zeros_like(acc_ref)

    up = jnp.dot(h_ref[...], wu_ref[...], preferred_element_type=f32)
    act = jnp.square(jnp.maximum(up, 0.0)).astype(bf16)
    acc_ref[...] += jnp.dot(act, wd_ref[...], preferred_element_type=f32)

    @pl.when(fstep == pl.num_programs(1) - 1)
    def _():
        x2 = x_ref[...] + _rms(acc_ref[...], g1_ref[...])
        xo_ref[...] = x2
        ho_ref[...] = _rms(x2, g2_ref[...]).astype(ho_ref.dtype)


def _ffn(h, w_up, w_down, x, g_post, g_next):
    m, d = h.shape
    dff = w_up.shape[1]
    tm, tf = _tile(m, 256), _tile(dff, 512)
    return pl.pallas_call(
        _ffn_kernel,
        out_shape=(jax.ShapeDtypeStruct((m, d), f32), jax.ShapeDtypeStruct((m, d), bf16)),
        grid=(m // tm, dff // tf),
        in_specs=[pl.BlockSpec((tm, d), lambda i, j: (i, 0)),
                  pl.BlockSpec((d, tf), lambda i, j: (0, j)),
                  pl.BlockSpec((tf, d), lambda i, j: (j, 0)),
                  pl.BlockSpec((tm, d), lambda i, j: (i, 0)),
                  pl.BlockSpec((1, d), lambda i, j: (0, 0)),
                  pl.BlockSpec((1, d), lambda i, j: (0, 0))],
        out_specs=(pl.BlockSpec((tm, d), lambda i, j: (i, 0)),
                   pl.BlockSpec((tm, d), lambda i, j: (i, 0))),
        scratch_shapes=[pltpu.VMEM((tm, d), f32)],
        compiler_params=_params("parallel", "arbitrary"),
        name="ffn",
    )(h, w_up, w_down, x, g_post, g_next)


def _ple_kernel(h_ref, wg_ref, p_ref, wp_ref, x_ref, g_ref, o_ref, acc_ref):
    kstep = pl.program_id(1)

    @pl.when(kstep == 0)
    def _():
        acc_ref[...] = jnp.zeros_like(acc_ref)

    acc_ref[...] += jnp.dot(h_ref[...], wg_ref[...], preferred_element_type=f32)

    @pl.when(kstep == pl.num_programs(1) - 1)
    def _():
        pe = jnp.dot(p_ref[...].astype(bf16), wp_ref[...], preferred_element_type=f32)
        o_ref[...] = x_ref[...] + _rms(pe * _sigmoid(acc_ref[...]), g_ref[...])


def _ple(h, w_gate, p, w_ple, x, g_post):
    m, k = h.shape
    n = w_gate.shape[1]
    q = p.shape[1]
    tm, tk = _tile(m, 256), _tile(k, 512)
    return pl.pallas_call(
        _ple_kernel,
        out_shape=jax.ShapeDtypeStruct((m, n), f32),
        grid=(m // tm, k // tk),
        in_specs=[pl.BlockSpec((tm, tk), lambda i, kk: (i, kk)),
                  pl.BlockSpec((tk, n), lambda i, kk: (kk, 0)),
                  pl.BlockSpec((tm, q), lambda i, kk: (i, 0)),
                  pl.BlockSpec((q, n), lambda i, kk: (0, 0)),
                  pl.BlockSpec((tm, n), lambda i, kk: (i, 0)),
                  pl.BlockSpec((1, n), lambda i, kk: (0, 0))],
        out_specs=pl.BlockSpec((tm, n), lambda i, kk: (i, 0)),
        scratch_shapes=[pltpu.VMEM((tm, n), f32)],
        compiler_params=_params("parallel", "arbitrary"),
        name="ple",
    )(h, w_gate, p, w_ple, x, g_post)


def _shift(x, bnd, mu, seg_len):
    nb, _, w = x.shape
    rows = nb * seg_len
    x2 = x.reshape(rows, w)
    prev = pltpu.roll(x2, 1, 0)
    first = (lax.broadcasted_iota(jnp.int32, (rows, w), 0) % seg_len) == 0
    bnd_rows = jnp.broadcast_to(bnd, (nb, seg_len, w)).reshape(rows, w)
    prev = jnp.where(first, bnd_rows, prev)
    return x2 + (prev - x2) * mu


def _prep_kernel(pr_ref, pk_ref, pv_ref, br_ref, bk_ref, bv_ref, mur_ref, muk_ref, muv_ref,
                 pl_ref, bl_ref, mul_ref, w0_ref, a0_ref, kkw_ref, kaw_ref,
                 w2_ref, a2_ref, g2_ref, e_ref,
                 r_ref, lw_ref, k_ref, v_ref, kk_ref, a_ref, g_ref,
                 tw_ref, al_ref, sg_ref, *, seg_len, lora):
    dl, al_n, gl_n = lora

    @pl.when(pl.program_id(1) == 0)
    def _():
        xl = _shift(pl_ref[...], bl_ref[...], mul_ref[...], seg_len)
        tw_ref[...] = jnp.tanh(xl[:, :dl]).astype(bf16)
        al_ref[...] = xl[:, dl:dl + al_n].astype(bf16)
        sg_ref[...] = _sigmoid(xl[:, dl + al_n:dl + al_n + gl_n]).astype(bf16)

    shape = r_ref.shape
    r = _shift(pr_ref[...], br_ref[...], mur_ref[...], seg_len)
    k = _shift(pk_ref[...], bk_ref[...], muk_ref[...], seg_len)
    v = _shift(pv_ref[...], bv_ref[...], muv_ref[...], seg_len)

    dec = w0_ref[...] + jnp.dot(tw_ref[...], w2_ref[...], preferred_element_type=f32)
    z = -dec
    softplus = jnp.maximum(z, 0.0) + jnp.log(1.0 + jnp.exp(-jnp.abs(z)))
    lw = -jnp.exp(-softplus - 0.5)
    a = _sigmoid(a0_ref[...] + jnp.dot(al_ref[...], a2_ref[...], preferred_element_type=f32))
    g = jnp.dot(sg_ref[...], g2_ref[...], preferred_element_type=f32)

    kk = k * kkw_ref[...]
    ssq = _dot_exact_rhs(kk * kk, e_ref[...])
    kk = kk / jnp.maximum(jnp.sqrt(ssq), KK_EPS)
    k = k * (1.0 + (a - 1.0) * kaw_ref[...])

    r_ref[...] = r.reshape(shape)
    lw_ref[...] = lw.reshape(shape)
    k_ref[...] = k.reshape(shape)
    v_ref[...] = v.reshape(shape)
    kk_ref[...] = kk.reshape(shape)
    a_ref[...] = a.reshape(shape)
    g_ref[...] = g.reshape(shape)


def _head_indicator(width):
    h = jnp.arange(width) // HEAD_SIZE
    return (h[:, None] == h[None, :]).astype(bf16)


def _prep(pa_rkv, bnd_rkv, pa_lora, bnd_lora, mu_rkv, mu_lora, w0, a0, k_k, k_a, w2, a2, g2,
          seg_per_tile):
    nseg, seg_len, c3 = pa_rkv.shape
    c = c3 // 3
    wl = pa_lora.shape[2]
    lora = (w2.shape[0], a2.shape[0], g2.shape[0])
    nb = seg_per_tile
    tc = _tile(c, 512)
    nj = c // tc
    rows = nb * seg_len
    tok = lambda off: pl.BlockSpec((nb, seg_len, tc), lambda i, j: (i, 0, j + off))
    bnd = lambda off: pl.BlockSpec((nb, 1, tc), lambda i, j: (i, 0, j + off))
    vec = lambda off: pl.BlockSpec((1, tc), lambda i, j: (0, j + off))
    low = lambda n: pl.BlockSpec((n, tc), lambda i, j: (0, j))
    out = jax.ShapeDtypeStruct((nseg, seg_len, c), f32)
    return pl.pallas_call(
        functools.partial(_prep_kernel, seg_len=seg_len, lora=lora),
        out_shape=(out,) * 7,
        grid=(nseg // nb, nj),
        in_specs=[tok(0), tok(nj), tok(2 * nj), bnd(0), bnd(nj), bnd(2 * nj), vec(0), vec(nj), vec(2 * nj),
                  pl.BlockSpec((nb, seg_len, wl), lambda i, j: (i, 0, 0)),
                  pl.BlockSpec((nb, 1, wl), lambda i, j: (i, 0, 0)),
                  pl.BlockSpec((1, wl), lambda i, j: (0, 0)),
                  vec(0), vec(0), vec(0), vec(0),
                  low(lora[0]), low(lora[1]), low(lora[2]),
                  pl.BlockSpec((tc, tc), lambda i, j: (0, 0))],
        out_specs=(pl.BlockSpec((nb, seg_len, tc), lambda i, j: (i, 0, j)),) * 7,
        scratch_shapes=[pltpu.VMEM((rows, lora[0]), bf16), pltpu.VMEM((rows, lora[1]), bf16),
                        pltpu.VMEM((rows, lora[2]), bf16)],
        compiler_params=_params("parallel", "arbitrary"),
        name="rwkv_prep",
    )(pa_rkv, pa_rkv, pa_rkv, bnd_rkv, bnd_rkv, bnd_rkv, mu_rkv, mu_rkv, mu_rkv,
      pa_lora, bnd_lora, mu_lora, w0, a0, k_k, k_a, w2, a2, g2, _head_indicator(tc))


def _bd(x, block):
    n = x.shape[1] // block
    lane_blk = lax.broadcasted_iota(jnp.int32, x.shape, 1) // block
    return jnp.concatenate([jnp.where(lane_blk == h, x, 0.0) for h in range(n)], axis=0)


def _diag_blocks(x, block):
    n = x.shape[1] // block
    rows = x.shape[0] // n
    lane_blk = lax.broadcasted_iota(jnp.int32, (rows, x.shape[1]), 1) // block
    out = x[0:rows]
    for h in range(1, n):
        out = jnp.where(lane_blk == h, x[h * rows:(h + 1) * rows], out)
    return out


def _mm(a, b, dims=_NN):
    return lax.dot_general(a.astype(bf16), b.astype(bf16), dims, preferred_element_type=f32)


def _wkv_local(ins, cls, masks, chunk):
    strict, incl, eye, eye_state = masks
    n = HEAD_SIZE
    gn = cls[0].shape[1]
    mw = (gn // n) * chunk
    cat = jnp.concatenate
    each = range(len(ins))

    pre = []
    for (r, lw, k, v, kk, a), cl in zip(ins, cls):
        cl_last = cl[chunk - 1:chunk, :]
        at = -kk * jnp.exp(cl - lw)
        rt = r * jnp.exp(cl)
        e_inv = jnp.exp(-cl)
        e_tail = jnp.exp(cl_last - cl)
        b = kk * a
        pre.append((at, rt, b * e_inv, k * e_inv, b * e_tail, k * e_tail, jnp.exp(cl_last)))

    d = [_mm(cat([at, rt], axis=0), cat([_bd(bt, n), _bd(kt, n)], axis=0), _NT)
         for at, rt, bt, kt, _, _, _ in pre]
    a_ab = [jnp.where(strict, x[:chunk, :mw], 0.0) for x in d]
    a_ak = [jnp.where(strict, x[:chunk, mw:], 0.0) for x in d]
    a_rr = [cat([jnp.where(incl, x[chunk:, :mw], 0.0), jnp.where(incl, x[chunk:, mw:], 0.0)], axis=1) for x in d]

    inv = [eye + x for x in a_ab]
    apow = a_ab
    apow_bd = [_bd(x, chunk).astype(bf16) for x in apow]
    for _ in range(chunk.bit_length() - 2):
        apow = [_mm(apow[i], apow_bd[i]) for i in each]
        apow_bd = [_bd(x, chunk).astype(bf16) for x in apow]
        inv = [inv[i] + _mm(inv[i], apow_bd[i]) for i in each]

    v_bd = [_bd(x[3], n).astype(bf16) for x in ins]
    rhs1 = [_mm(a_ak[i], v_bd[i]) for i in each]
    x = [_mm(inv[i], cat([_bd(rhs1[i], n), _bd(pre[i][0], n)], axis=1)) for i in each]
    u_loc = [z[:, :gn] for z in x]
    gh = [z[:, gn:] for z in x]
    yq = [_mm(a_rr[i], cat([cat([_bd(u_loc[i], n), _bd(gh[i], n)], axis=1).astype(bf16),
                            cat([v_bd[i], jnp.zeros_like(v_bd[i])], axis=1)], axis=0)) for i in each]
    ms = [_mm(cat([pre[i][4], pre[i][5]], axis=0),
              cat([cat([gh[i], u_loc[i]], axis=1), cat([jnp.zeros_like(ins[i][3]), ins[i][3]], axis=1)], axis=0), _TN)
          for i in each]
    y_loc = [z[:, :gn] for z in yq]
    q = [pre[i][1] + yq[i][:, gn:] for i in each]
    mt = [_diag_blocks(ms[i][:, :gn], n) + eye_state * pre[i][6] for i in each]
    sloc = [_diag_blocks(z[:, gn:], n) for z in ms]
    return y_loc, q, mt, sloc


def _wkv_kernel(r_ref, lw_ref, k_ref, v_ref, kk_ref, a_ref, g_ref, rk_ref, lng_ref, lnb_ref, s0_ref,
                o_ref, st_ref, state_ref, *, chunk):
    n = HEAD_SIZE
    nb, lt, width = r_ref.shape
    gn = (128 // chunk) * n
    tstep = pl.program_id(2)

    @pl.when(tstep == 0)
    def _():
        state_ref[...] = s0_ref[...]

    mrow = lax.broadcasted_iota(jnp.int32, (chunk, 128), 0)
    mcol = lax.broadcasted_iota(jnp.int32, (chunk, 128), 1) % chunk
    srow = lax.broadcasted_iota(jnp.int32, (n, gn), 0)
    scol = lax.broadcasted_iota(jnp.int32, (n, gn), 1) % n
    masks = (mrow > mcol, mrow >= mcol, (mrow == mcol).astype(f32), (srow == scol).astype(f32))
    trow = lax.broadcasted_iota(jnp.int32, (chunk, chunk), 0)
    tcol = lax.broadcasted_iota(jnp.int32, (chunk, chunk), 1)
    tri = (trow >= tcol).astype(bf16)
    irow = lax.broadcasted_iota(jnp.int32, (gn, gn), 0) // n
    icol = lax.broadcasted_iota(jnp.int32, (gn, gn), 1) // n
    head_ind = (irow == icol).astype(bf16)

    chains = [(bi, p) for bi in range(nb) for p in range(width // gn)]
    nchunk = lt // chunk
    probs = [(bi, p, ci) for bi, p in chains for ci in range(nchunk)]
    where = {pr: i for i, pr in enumerate(probs)}

    def index(bi, p, ci):
        return bi, slice(ci * chunk, (ci + 1) * chunk), slice(p * gn, (p + 1) * gn)

    ins = [tuple(ref[index(*pr)] for ref in (r_ref, lw_ref, k_ref, v_ref, kk_ref, a_ref)) for pr in probs]
    halves = [_split(x[1]) for x in ins]
    cls = [lax.dot_general(tri, hi, _NN, preferred_element_type=f32)
           + lax.dot_general(tri, lo, _NN, preferred_element_type=f32) for hi, lo in halves]
    y_loc, q, mt, sloc = _wkv_local(ins, cls, masks, chunk)

    st = {ch: state_ref[ch] for ch in chains}
    ys, stats = {}, {}
    for step in range(nchunk + 2):
        if step < nchunk:
            for ch in chains:
                i = where[ch + (step,)]
                mt_hi, mt_lo = _split(mt[i])
                r2 = lax.dot_general(jnp.concatenate([q[i].astype(bf16), mt_hi, mt_lo], axis=0),
                                     _bd(st[ch], n).astype(bf16), _NN, preferred_element_type=f32)
                ys[ch, step] = r2[:chunk] + y_loc[i]
                st[ch] = r2[chunk:chunk + n] + r2[chunk + n:] + sloc[i]
        if 0 <= step - 1 < nchunk:
            for ch in chains:
                idx = index(*ch, step - 1)
                y_hi, y_lo = _split(ys[ch, step - 1])
                bonus_w = (r_ref[idx] * k_ref[idx] * rk_ref[:, idx[2]]).astype(bf16)
                stats[ch, step - 1] = lax.dot_general(jnp.concatenate([y_hi, y_lo, bonus_w], axis=0), head_ind,
                                                      _NN, preferred_element_type=f32)
        if 0 <= step - 2 < nchunk:
            for ch in chains:
                idx = index(*ch, step - 2)
                lanes = idx[2]
                sm = stats.pop((ch, step - 2))
                dlt = ys.pop((ch, step - 2)) - (sm[:chunk] + sm[chunk:2 * chunk]) * (1.0 / n)
                var = lax.dot_general((dlt * dlt).astype(bf16), head_ind, _NN,
                                      preferred_element_type=f32) * (1.0 / n)
                out = (dlt * lax.rsqrt(var + GN_EPS) * lng_ref[:, lanes] + lnb_ref[:, lanes]
                       + sm[2 * chunk:] * v_ref[idx])
                o_ref[idx] = (out * g_ref[idx]).astype(o_ref.dtype)
    for ch in chains:
        state_ref[ch] = st[ch]

    @pl.when(tstep == pl.num_programs(2) - 1)
    def _():
        st_ref[...] = state_ref[...]


def _wkv(r, lw, k, v, kk, a, g, r_k, lnx_g, lnx_b, s0, chunk, rows_per_step, seq_per_step, lanes_per_step):
    bsz, t, c = r.shape
    n = HEAD_SIZE
    grp = 128 // chunk
    gn = grp * n
    heads = c // n
    nb, lt, width = seq_per_step, rows_per_step, lanes_per_step
    pg = width // gn
    s0k = s0.reshape(bsz, heads // grp, grp, n, n).transpose(0, 1, 4, 2, 3).reshape(bsz, heads // grp, n, gn)
    tok = pl.BlockSpec((nb, lt, width), lambda b, gi, ti: (b, ti, gi))
    vec = pl.BlockSpec((1, width), lambda b, gi, ti: (0, gi))
    st = pl.BlockSpec((nb, pg, n, gn), lambda b, gi, ti: (b, gi, 0, 0))
    out, stk = pl.pallas_call(
        functools.partial(_wkv_kernel, chunk=chunk),
        out_shape=(jax.ShapeDtypeStruct((bsz, t, c), bf16), jax.ShapeDtypeStruct(s0k.shape, f32)),
        grid=(bsz // nb, c // width, t // lt),
        in_specs=[tok] * 7 + [vec] * 3 + [st],
        out_specs=(tok, st),
        scratch_shapes=[pltpu.VMEM((nb, pg, n, gn), f32)],
        compiler_params=_params("parallel", "parallel", "arbitrary"),
        name="wkv",
    )(r, lw, k, v, kk, a, g, r_k, lnx_g, lnx_b, s0k)
    s_new = stk.reshape(bsz, heads // grp, n, grp, n).transpose(0, 1, 3, 4, 2).reshape(bsz, heads, n, n)
    return out, s_new


def _conv_kernel(u_ref, halo_ref, st_ref, w_ref, b_ref, o_ref, ext_ref, *, taps):
    nb, tt, _ = u_ref.shape
    first = pl.program_id(1) == 0

    @pl.when(first)
    def _():
        ext_ref[:, 0:CONV_HALO, :] = st_ref[...]

    @pl.when(jnp.logical_not(first))
    def _():
        ext_ref[:, 0:CONV_HALO, :] = halo_ref[...]

    ext_ref[:, CONV_HALO:CONV_HALO + tt, :] = u_ref[...]
    lead = CONV_HALO - (taps - 1)
    acc = jnp.zeros(u_ref.shape, f32) + b_ref[...]
    for tap in range(taps):
        acc = acc + w_ref[tap:tap + 1, :] * ext_ref[:, lead + tap:lead + tap + tt, :]
    o_ref[...] = acc


def _conv(u, hist, dw_w, dw_b, seq_per_tile, rows_per_tile):
    bsz, t, c = u.shape
    nb, tt = seq_per_tile, rows_per_tile
    tc = _tile(c, 256)
    hpt = tt // CONV_HALO
    taps = dw_w.shape[0]
    return pl.pallas_call(
        functools.partial(_conv_kernel, taps=taps),
        out_shape=jax.ShapeDtypeStruct((bsz, t, c), f32),
        grid=(bsz // nb, t // tt, c // tc),
        in_specs=[pl.BlockSpec((nb, tt, tc), lambda b, i, j: (b, i, j)),
                  pl.BlockSpec((nb, CONV_HALO, tc), lambda b, i, j: (b, jnp.maximum(i * hpt - 1, 0), j)),
                  pl.BlockSpec((nb, CONV_HALO, tc), lambda b, i, j: (b, 0, j)),
                  pl.BlockSpec((taps, tc), lambda b, i, j: (0, j)),
                  pl.BlockSpec((1, tc), lambda b, i, j: (0, j))],
        out_specs=pl.BlockSpec((nb, tt, tc), lambda b, i, j: (b, i, j)),
        scratch_shapes=[pltpu.VMEM((nb, CONV_HALO + tt, tc), f32)],
        compiler_params=_params("parallel", "arbitrary", "arbitrary"),
        name="dwconv",
    )(u, u, hist, dw_w, dw_b)


def _ln_silu_kernel(x_ref, g_ref, b_ref, o_ref):
    x = x_ref[...]
    mu = jnp.mean(x, axis=-1, keepdims=True)
    d = x - mu
    var = jnp.mean(d * d, axis=-1, keepdims=True)
    y = d * lax.rsqrt(var + LN_EPS) * g_ref[...] + b_ref[...]
    o_ref[...] = (y * _sigmoid(y)).astype(o_ref.dtype)


def _ln_silu(x, g, b):
    m, d = x.shape
    tm = _tile(m, 512)
    return pl.pallas_call(
        _ln_silu_kernel,
        out_shape=jax.ShapeDtypeStruct((m, d), bf16),
        grid=(m // tm,),
        in_specs=[pl.BlockSpec((tm, d), lambda i: (i, 0)), pl.BlockSpec((1, d), lambda i: (0, 0)),
                  pl.BlockSpec((1, d), lambda i: (0, 0))],
        out_specs=pl.BlockSpec((tm, d), lambda i: (i, 0)),
        compiler_params=_params("parallel"),
        name="ln_silu",
    )(x, g, b)


def _layer(x, p, s_wkv, s_shift, s_conv, w, *, chunk, seg_len):
    bsz, t, d = x.shape
    m = bsz * t
    c = w["w_o_a"].shape[0]
    c3 = 3 * c
    n_shift = s_shift.shape[1]
    n_lora = n_shift - c3
    x2 = x.reshape(m, d)

    h = _rms_cast(x2, w["g_pre_mix"])
    pa_rkv = _matmul(h, w["w_in_rkv"], name="mm_rkv")
    pa_lora = _matmul(h, w["w_in_lora"], tn_pref=w["w_in_lora"].shape[1], name="mm_lora")
    u = _matmul_glu(h, w["w_in_conv"])
    gates = _matmul(h, w["w_in_gate"], act="sigmoid", name="mm_gate")

    spb = t // seg_len
    nseg = bsz * spb
    lw_pad = pa_lora.shape[1]

    def boundaries(pa, carried):
        width = pa.shape[1]
        tails = pa.reshape(bsz, spb, seg_len, width)[:, :-1, seg_len - 1, :]
        return jnp.concatenate([carried[:, None, :], tails], axis=1).reshape(nseg, 1, width)

    bnd_rkv = boundaries(pa_rkv, s_shift[:, :c3])
    bnd_lora = boundaries(pa_lora, jnp.pad(s_shift[:, c3:], ((0, 0), (0, lw_pad - n_lora))))
    seg_per_tile = _tile(nseg, max(1, 256 // seg_len))
    prep = _prep(pa_rkv.reshape(nseg, seg_len, c3), bnd_rkv, pa_lora.reshape(nseg, seg_len, lw_pad), bnd_lora,
                 w["mu_rkv"], w["mu_lora"], w["w0"], w["a0"], w["k_k"], w["k_a"], w["w2"], w["a2"], w["g2"],
                 seg_per_tile)
    r, lw, k, v, kk, a, g = (z.reshape(bsz, t, c) for z in prep)
    wkv_rows = _tile(t, 256)
    wkv_lanes = max(256, (128 // chunk) * HEAD_SIZE)
    problems = (wkv_rows // chunk) * (wkv_lanes // ((128 // chunk) * HEAD_SIZE))
    ya_in, s_new = _wkv(r, lw, k, v, kk, a, g, w["r_k"], w["lnx_g"], w["lnx_b"], s_wkv, chunk=chunk,
                        rows_per_step=wkv_rows, seq_per_step=_tile(bsz, max(1, 8 // problems)),
                        lanes_per_step=wkv_lanes)

    last = (jnp.arange(bsz) + 1) * t - 1
    shift_new = jnp.concatenate([pa_rkv[last], pa_lora[last, :n_lora]], axis=1)

    u3 = u.reshape(bsz, t, c)
    hist = jnp.pad(s_conv, ((0, 0), (CONV_HALO - s_conv.shape[1], 0), (0, 0)))
    conv_rows = _tile(t, 256)
    cv = _conv(u3, hist, w["dw_w"], w["dw_b"], _tile(bsz, max(1, 256 // conv_rows)), conv_rows)
    cv = _ln_silu(cv.reshape(m, c), w["lnc_g"], w["lnc_b"])
    keep = s_conv.shape[1]
    conv_new = jnp.concatenate([s_conv, u3], axis=1)[:, -keep:] if t < keep else u3[:, t - keep:]

    merged = _matmul_merge(ya_in.reshape(m, c), cv, w["w_o_a"], w["w_o_b"], gates)
    x1, hf = _out_proj(merged, w["w_out"], x2, w["g_post_mix"], w["g_pre_ffn"])
    x2_, hp = _ffn(hf, w["w_up"], w["w_down"], x1, w["g_post_ffn"], w["g_pre_ple"])
    y = _ple(hp, w["w_ple_gate"], p.reshape(m, p.shape[-1]), w["w_ple"], x2_, w["g_post_ple"])
    return y.reshape(bsz, t, d), s_new, shift_new, conv_new


def _layer_weights(i, c, g_pre_mix, w_in, mu_shift, w0, w2, a0, a2, g2, k_k, k_a, r_k, lnx_g, lnx_b, w_o_a,
                   dw_w, dw_b, lnc_g, lnc_b, w_o_b, w_out, g_post_mix, g_pre_ffn, w_up, w_down, g_post_ffn,
                   g_pre_ple, w_ple_gate, w_ple, g_post_ple):
    c3 = 3 * c
    n_lora = w2.shape[1] + a2.shape[1] + g2.shape[1]
    n_shift = c3 + n_lora
    lw_pad = -(-n_lora // 128) * 128
    row = lambda z: z[i].reshape(1, -1)
    win = w_in[i]
    return {
        "g_pre_mix": row(g_pre_mix),
        "w_in_rkv": win[:, :c3].astype(bf16),
        "w_in_lora": jnp.pad(win[:, c3:n_shift], ((0, 0), (0, lw_pad - n_lora))).astype(bf16),
        "w_in_conv": win[:, n_shift:n_shift + 2 * c].astype(bf16),
        "w_in_gate": win[:, n_shift + 2 * c:].astype(bf16),
        "mu_rkv": mu_shift[i, :c3].reshape(1, -1),
        "mu_lora": jnp.pad(mu_shift[i, c3:], (0, lw_pad - n_lora)).reshape(1, -1),
        "w0": row(w0), "a0": row(a0), "k_k": row(k_k), "k_a": row(k_a),
        "w2": w2[i].astype(bf16), "a2": a2[i].astype(bf16), "g2": g2[i].astype(bf16),
        "r_k": row(r_k), "lnx_g": row(lnx_g), "lnx_b": row(lnx_b),
        "w_o_a": w_o_a[i].astype(bf16), "w_o_b": w_o_b[i].astype(bf16), "w_out": w_out[i].astype(bf16),
        "dw_w": dw_w[i], "dw_b": row(dw_b), "lnc_g": row(lnc_g), "lnc_b": row(lnc_b),
        "g_post_mix": row(g_post_mix), "g_pre_ffn": row(g_pre_ffn),
        "w_up": w_up[i].astype(bf16), "w_down": w_down[i].astype(bf16),
        "g_post_ffn": row(g_post_ffn), "g_pre_ple": row(g_pre_ple),
        "w_ple_gate": w_ple_gate[i].astype(bf16), "w_ple": w_ple[i].astype(bf16),
        "g_post_ple": row(g_post_ple),
    }


def kernel(x_prompt, x_sample, p_prompt, p_sample, state_wkv, state_shift, state_conv, g_pre_mix, w_in, mu_shift, w0, w2, a0, a2, g2, k_k, k_a, r_k, lnx_g, lnx_b, w_o_a, dw_w, dw_b, lnc_g, lnc_b, w_o_b, w_out, g_post_mix, g_pre_ffn, w_up, w_down, g_post_ffn, g_pre_ple, w_ple_gate, w_ple, g_post_ple):
    depth = w_in.shape[0]
    c = w_o_a.shape[1]
    heads = c // HEAD_SIZE
    bp, tp, _ = x_prompt.shape
    ts = x_sample.shape[1]
    xp, xs = x_prompt, x_sample
    outs = [[] for _ in range(6)]
    for i in range(depth):
        w = _layer_weights(i, c, g_pre_mix, w_in, mu_shift, w0, w2, a0, a2, g2, k_k, k_a, r_k, lnx_g, lnx_b,
                           w_o_a, dw_w, dw_b, lnc_g, lnc_b, w_o_b, w_out, g_post_mix, g_pre_ffn, w_up, w_down,
                           g_post_ffn, g_pre_ple, w_ple_gate, w_ple, g_post_ple)
        xp, s1, s2, s3 = _layer(xp, p_prompt[i],
                                jnp.zeros((bp, heads, HEAD_SIZE, HEAD_SIZE), f32),
                                jnp.zeros((bp, state_shift.shape[2]), f32),
                                jnp.zeros((bp,) + state_conv.shape[2:], f32), w,
                                chunk=min(64, tp), seg_len=_tile(tp, 256))
        xs, t1, t2, t3 = _layer(xs, p_sample[i], state_wkv[i], state_shift[i], state_conv[i], w,
                                chunk=min(64, ts), seg_len=ts)
        for lst, val in zip(outs, (s1, s2, s3, t1, t2, t3)):
            lst.append(val)
    return (xp, xs) + tuple(jnp.stack(o) for o in outs)
```

```python
import functools

import jax
import jax.numpy as jnp
from jax import lax
from jax.experimental import pallas as pl
from jax.experimental.pallas import tpu as pltpu

f32 = jnp.float32
bf16 = jnp.bfloat16

HEAD_SIZE = 64
CONV_K = 31
CONV_HALO = 32
RMS_EPS = 1e-6
LN_EPS = 1e-5
GN_EPS = 64e-5
KK_EPS = 1e-12
VMEM_LIMIT = 56 * 1024 * 1024

_NN = (((1,), (0,)), ((), ()))
_NT = (((1,), (1,)), ((), ()))
_TN = (((0,), (0,)), ((), ()))


def _params(*sem):
    return pltpu.CompilerParams(dimension_semantics=sem, vmem_limit_bytes=VMEM_LIMIT)


def _tile(n, pref):
    t = min(n, pref)
    while n % t:
        t -= 1
    return t


def _dot(a, b, dims=_NN):
    return lax.dot_general(a.astype(bf16), b.astype(bf16), dims, preferred_element_type=f32)


def _split(a):
    hi = a.astype(bf16)
    lo = (a - hi.astype(f32)).astype(bf16)
    return hi, lo


def _dot3(a, b, dims=_NN):
    a_hi, a_lo = _split(a)
    b_hi, b_lo = _split(b)
    d = functools.partial(lax.dot_general, dimension_numbers=dims, preferred_element_type=f32)
    return d(a_hi, b_hi) + (d(a_hi, b_lo) + d(a_lo, b_hi))


def _dot_exact_rhs(a, b_bf16, dims=_NN):
    a_hi, a_lo = _split(a)
    d = functools.partial(lax.dot_general, dimension_numbers=dims, preferred_element_type=f32)
    return d(a_hi, b_bf16) + d(a_lo, b_bf16)


def _sigmoid(x):
    return 1.0 / (1.0 + jnp.exp(-x))


def _rms(x, g):
    return x * lax.rsqrt(jnp.mean(x * x, axis=-1, keepdims=True) + RMS_EPS) * g


def _rms_cast_kernel(x_ref, g_ref, o_ref):
    o_ref[...] = _rms(x_ref[...], g_ref[...]).astype(o_ref.dtype)


def _rms_cast(x, g):
    m, d = x.shape
    tm = _tile(m, 512)
    return pl.pallas_call(
        _rms_cast_kernel,
        out_shape=jax.ShapeDtypeStruct((m, d), bf16),
        grid=(m // tm,),
        in_specs=[pl.BlockSpec((tm, d), lambda i: (i, 0)), pl.BlockSpec((1, d), lambda i: (0, 0))],
        out_specs=pl.BlockSpec((tm, d), lambda i: (i, 0)),
        compiler_params=_params("parallel"),
        name="rms_cast",
    )(x, g)


def _mm_kernel(a_ref, b_ref, o_ref, *, act):
    acc = jnp.dot(a_ref[...], b_ref[...].astype(bf16), preferred_element_type=f32)
    if act == "sigmoid":
        acc = _sigmoid(acc)
    o_ref[...] = acc.astype(o_ref.dtype)


def _matmul(a, b, n, col0=0, act=None, out_dtype=f32, tm_pref=512, tn_pref=1024, name="mm"):
    m, k = a.shape
    tm, tn = _tile(m, tm_pref), _tile(n, tn_pref)
    j0 = col0 // tn
    assert j0 * tn == col0
    return pl.pallas_call(
        functools.partial(_mm_kernel, act=act),
        out_shape=jax.ShapeDtypeStruct((m, n), out_dtype),
        grid=(m // tm, n // tn),
        in_specs=[pl.BlockSpec((tm, k), lambda i, j: (i, 0)), pl.BlockSpec((k, tn), lambda i, j: (0, j + j0))],
        out_specs=pl.BlockSpec((tm, tn), lambda i, j: (i, j)),
        compiler_params=_params("parallel", "arbitrary"),
        name=name,
    )(a, b)


def _glu_kernel(a_ref, b1_ref, b2_ref, o_ref):
    a = a_ref[...]
    lin = jnp.dot(a, b1_ref[...], preferred_element_type=f32)
    gate = jnp.dot(a, b2_ref[...], preferred_element_type=f32)
    o_ref[...] = lin * _sigmoid(gate)


def _matmul_glu(a, b, n):
    m, k = a.shape
    tm, tn = _tile(m, 512), _tile(n, 512)
    nj = n // tn
    return pl.pallas_call(
        _glu_kernel,
        out_shape=jax.ShapeDtypeStruct((m, n), f32),
        grid=(m // tm, nj),
        in_specs=[pl.BlockSpec((tm, k), lambda i, j: (i, 0)),
                  pl.BlockSpec((k, tn), lambda i, j: (0, j)),
                  pl.BlockSpec((k, tn), lambda i, j: (0, j + nj))],
        out_specs=pl.BlockSpec((tm, tn), lambda i, j: (i, j)),
        compiler_params=_params("parallel", "arbitrary"),
        name="mm_glu",
    )(a, b, b)


def _merge_kernel(a1_ref, a2_ref, b1_ref, b2_ref, ga_ref, gb_ref, o_ref):
    ya = jnp.dot(a1_ref[...], b1_ref[...], preferred_element_type=f32)
    yb = jnp.dot(a2_ref[...], b2_ref[...], preferred_element_type=f32)
    o_ref[...] = (ga_ref[...] * ya + gb_ref[...] * yb).astype(o_ref.dtype)


def _matmul_merge(a1, a2, b1, b2, gates):
    m, k = a1.shape
    n = b1.shape[1]
    tm, tn = _tile(m, 512), _tile(n, 512)
    nj = n // tn
    return pl.pallas_call(
        _merge_kernel,
        out_shape=jax.ShapeDtypeStruct((m, n), bf16),
        grid=(m // tm, nj),
        in_specs=[pl.BlockSpec((tm, k), lambda i, j: (i, 0)),
                  pl.BlockSpec((tm, k), lambda i, j: (i, 0)),
                  pl.BlockSpec((k, tn), lambda i, j: (0, j)),
                  pl.BlockSpec((k, tn), lambda i, j: (0, j)),
                  pl.BlockSpec((tm, tn), lambda i, j: (i, j)),
                  pl.BlockSpec((tm, tn), lambda i, j: (i, j + nj))],
        out_specs=pl.BlockSpec((tm, tn), lambda i, j: (i, j)),
        compiler_params=_params("parallel", "arbitrary"),
        name="mm_merge",
    )(a1, a2, b1, b2, gates, gates)


ROW_BLOCK = 256


def _row_blocks(ref):
    sub = _tile(ref.shape[0], ROW_BLOCK)
    return [slice(r, r + sub) for r in range(0, ref.shape[0], sub)]


def _accumulate(o_ref, step, part):
    @pl.when(step == 0)
    def _():
        for rows in _row_blocks(o_ref):
            o_ref[rows, :] = part(rows)

    @pl.when(step > 0)
    def _():
        for rows in _row_blocks(o_ref):
            o_ref[rows, :] += part(rows)


_ONCE = pl.Buffered(1)


def _out_proj_kernel(a_ref, b_ref, x_ref, g1_ref, g2_ref, xo_ref, ho_ref):
    kstep = pl.program_id(1)
    _accumulate(xo_ref, kstep, lambda rows: jnp.dot(a_ref[rows, :], b_ref[...], preferred_element_type=f32))

    @pl.when(kstep == pl.num_programs(1) - 1)
    def _():
        for rows in _row_blocks(xo_ref):
            x1 = x_ref[rows, :] + _rms(xo_ref[rows, :], g1_ref[...])
            xo_ref[rows, :] = x1
            ho_ref[rows, :] = _rms(x1, g2_ref[...]).astype(ho_ref.dtype)


def _out_proj(a, b, x, g_post, g_next):
    m, k = a.shape
    n = b.shape[1]
    tm, tk = _tile(m, 512), _tile(k, 512)
    return pl.pallas_call(
        _out_proj_kernel,
        out_shape=(jax.ShapeDtypeStruct((m, n), f32), jax.ShapeDtypeStruct((m, n), bf16)),
        grid=(m // tm, k // tk),
        in_specs=[pl.BlockSpec((tm, tk), lambda i, kk: (i, kk)),
                  pl.BlockSpec((tk, n), lambda i, kk: (kk, 0)),
                  pl.BlockSpec((tm, n), lambda i, kk: (i, 0), pipeline_mode=_ONCE),
                  pl.BlockSpec((1, n), lambda i, kk: (0, 0)),
                  pl.BlockSpec((1, n), lambda i, kk: (0, 0))],
        out_specs=(pl.BlockSpec((tm, n), lambda i, kk: (i, 0)),
                   pl.BlockSpec((tm, n), lambda i, kk: (i, 0))),
        compiler_params=_params("parallel", "arbitrary"),
        name="out_proj",
    )(a, b, x, g_post, g_next)


def _ffn_kernel(h_ref, wu_ref, wd_ref, x_ref, g1_ref, g2_ref, xo_ref, ho_ref):
    fstep = pl.program_id(1)

    def part(rows):
        up = jnp.dot(h_ref[rows, :], wu_ref[...], preferred_element_type=f32)
        act = jnp.square(jnp.maximum(up, 0.0)).astype(bf16)
        return jnp.dot(act, wd_ref[...], preferred_element_type=f32)

    _accumulate(xo_ref, fstep, part)

    @pl.when(fstep == pl.num_programs(1) - 1)
    def _():
        for rows in _row_blocks(xo_ref):
            x2 = x_ref[rows, :] + _rms(xo_ref[rows, :], g1_ref[...])
            xo_ref[rows, :] = x2
            ho_ref[rows, :] = _rms(x2, g2_ref[...]).astype(ho_ref.dtype)


def _ffn(h, w_up, w_down, x, g_post, g_next):
    m, d = h.shape
    dff = w_up.shape[1]
    tm, tf = _tile(m, 512), _tile(dff, 512)
    return pl.pallas_call(
        _ffn_kernel,
        out_shape=(jax.ShapeDtypeStruct((m, d), f32), jax.ShapeDtypeStruct((m, d), bf16)),
        grid=(m // tm, dff // tf),
        in_specs=[pl.BlockSpec((tm, d), lambda i, j: (i, 0), pipeline_mode=_ONCE),
                  pl.BlockSpec((d, tf), lambda i, j: (0, j)),
                  pl.BlockSpec((tf, d), lambda i, j: (j, 0)),
                  pl.BlockSpec((tm, d), lambda i, j: (i, 0), pipeline_mode=_ONCE),
                  pl.BlockSpec((1, d), lambda i, j: (0, 0)),
                  pl.BlockSpec((1, d), lambda i, j: (0, 0))],
        out_specs=(pl.BlockSpec((tm, d), lambda i, j: (i, 0)),
                   pl.BlockSpec((tm, d), lambda i, j: (i, 0))),
        compiler_params=_params("parallel", "arbitrary"),
        name="ffn",
    )(h, w_up, w_down, x, g_post, g_next)


def _ple_kernel(h_ref, wg_ref, p_ref, wp_ref, x_ref, g_ref, o_ref):
    kstep = pl.program_id(1)
    _accumulate(o_ref, kstep, lambda rows: jnp.dot(h_ref[rows, :], wg_ref[...], preferred_element_type=f32))

    @pl.when(kstep == pl.num_programs(1) - 1)
    def _():
        for rows in _row_blocks(o_ref):
            pe = jnp.dot(p_ref[rows, :].astype(bf16), wp_ref[...], preferred_element_type=f32)
            o_ref[rows, :] = x_ref[rows, :] + _rms(pe * _sigmoid(o_ref[rows, :]), g_ref[...])


def _ple(h, w_gate, p, w_ple, x, g_post):
    m, k = h.shape
    n = w_gate.shape[1]
    q = p.shape[1]
    tm, tk = _tile(m, 512), _tile(k, 512)
    return pl.pallas_call(
        _ple_kernel,
        out_shape=jax.ShapeDtypeStruct((m, n), f32),
        grid=(m // tm, k // tk),
        in_specs=[pl.BlockSpec((tm, tk), lambda i, kk: (i, kk)),
                  pl.BlockSpec((tk, n), lambda i, kk: (kk, 0)),
                  pl.BlockSpec((tm, q), lambda i, kk: (i, 0)),
                  pl.BlockSpec((q, n), lambda i, kk: (0, 0), pipeline_mode=_ONCE),
                  pl.BlockSpec((tm, n), lambda i, kk: (i, 0), pipeline_mode=_ONCE),
                  pl.BlockSpec((1, n), lambda i, kk: (0, 0))],
        out_specs=pl.BlockSpec((tm, n), lambda i, kk: (i, 0)),
        compiler_params=_params("parallel", "arbitrary"),
        name="ple",
    )(h, w_gate, p, w_ple, x, g_post)


def _shift(x_ref, carried_ref, above_ref, mu_ref, seg_len, starts_sequence):
    x = x_ref[...]
    rows, w = x.shape
    prev = pltpu.roll(x, 1, 0)
    first = (lax.broadcasted_iota(jnp.int32, (rows, w), 0) % seg_len) == 0
    nseq = carried_ref.shape[0]
    if nseq == 1:
        above = above_ref[above_ref.shape[0] - 1:, :]
        bnd_rows = jnp.broadcast_to(jnp.where(starts_sequence, carried_ref[0], above), (rows, w))
    else:
        bnd_rows = jnp.broadcast_to(carried_ref[...], (nseq, seg_len, w)).reshape(rows, w)
    prev = jnp.where(first, bnd_rows, prev)
    return x + (prev - x) * mu_ref[...]


def _prep_kernel(pr_ref, pk_ref, pv_ref, cr_ref, ck_ref, cv_ref, ar_ref, ak_ref, av_ref, mur_ref, muk_ref, muv_ref,
                 pl_ref, cl_ref, al8_ref, mul_ref, w0_ref, a0_ref, kkw_ref, kaw_ref,
                 w2_ref, a2_ref, g2_ref, e_ref,
                 r_ref, lw_ref, k_ref, v_ref, kk_ref, a_ref, g_ref,
                 tw_ref, al_ref, sg_ref, *, seg_len, tiles_per_seq, lora):
    dl, al_n, gl_n = lora
    starts = (pl.program_id(0) % tiles_per_seq) == 0

    @pl.when(pl.program_id(1) == 0)
    def _():
        xl = _shift(pl_ref, cl_ref, al8_ref, mul_ref, seg_len, starts)
        tw_ref[...] = jnp.tanh(xl[:, :dl]).astype(bf16)
        al_ref[...] = xl[:, dl:dl + al_n].astype(bf16)
        sg_ref[...] = _sigmoid(xl[:, dl + al_n:dl + al_n + gl_n]).astype(bf16)

    r = _shift(pr_ref, cr_ref, ar_ref, mur_ref, seg_len, starts)
    k = _shift(pk_ref, ck_ref, ak_ref, muk_ref, seg_len, starts)
    v = _shift(pv_ref, cv_ref, av_ref, muv_ref, seg_len, starts)

    dec = w0_ref[...] + jnp.dot(tw_ref[...], w2_ref[...], preferred_element_type=f32)
    z = -dec
    softplus = jnp.maximum(z, 0.0) + jnp.log(1.0 + jnp.exp(-jnp.abs(z)))
    lw = -jnp.exp(-softplus - 0.5)
    a = _sigmoid(a0_ref[...] + jnp.dot(al_ref[...], a2_ref[...], preferred_element_type=f32))
    g = jnp.dot(sg_ref[...], g2_ref[...], preferred_element_type=f32)

    kk = k * kkw_ref[...]
    ssq = _dot_exact_rhs(kk * kk, e_ref[...])
    kk = kk / jnp.maximum(jnp.sqrt(ssq), KK_EPS)
    k = k * (1.0 + (a - 1.0) * kaw_ref[...])

    r_ref[...] = r
    lw_ref[...] = lw
    k_ref[...] = k
    v_ref[...] = v
    kk_ref[...] = kk
    a_ref[...] = a
    g_ref[...] = g


def _head_indicator(width):
    h = jnp.arange(width) // HEAD_SIZE
    return (h[:, None] == h[None, :]).astype(bf16)


def _prep(pa_rkv, carried_rkv, pa_lora, carried_lora, mu_rkv, mu_lora, w0, a0, k_k, k_a, w2, a2, g2, t):
    m, c3 = pa_rkv.shape
    c = c3 // 3
    wl = pa_lora.shape[1]
    lora = (w2.shape[0], a2.shape[0], g2.shape[0])
    tm = _tile(m, 256)
    seg_len = min(t, tm)
    nseq, tiles_per_seq = tm // seg_len, t // seg_len
    tc = _tile(c, 512)
    nj = c // tc
    sub = 8
    above_row = lambda i: jnp.maximum(i * (tm // sub) - 1, 0)
    tok = lambda off: pl.BlockSpec((tm, tc), lambda i, j: (i, j + off))
    car = lambda off: pl.BlockSpec((nseq, 1, tc), lambda i, j: (i // tiles_per_seq, 0, j + off))
    abv = lambda off: pl.BlockSpec((sub, tc), lambda i, j: (above_row(i), j + off))
    vec = lambda off: pl.BlockSpec((1, tc), lambda i, j: (0, j + off))
    low = lambda n: pl.BlockSpec((n, tc), lambda i, j: (0, j))
    out = jax.ShapeDtypeStruct((m, c), f32)
    return pl.pallas_call(
        functools.partial(_prep_kernel, seg_len=seg_len, tiles_per_seq=tiles_per_seq, lora=lora),
        out_shape=(out,) * 7,
        grid=(m // tm, nj),
        in_specs=[tok(0), tok(nj), tok(2 * nj), car(0), car(nj), car(2 * nj), abv(0), abv(nj), abv(2 * nj),
                  vec(0), vec(nj), vec(2 * nj),
                  pl.BlockSpec((tm, wl), lambda i, j: (i, 0)),
                  pl.BlockSpec((nseq, 1, wl), lambda i, j: (i // tiles_per_seq, 0, 0)),
                  pl.BlockSpec((sub, wl), lambda i, j: (above_row(i), 0)),
                  pl.BlockSpec((1, wl), lambda i, j: (0, 0)),
                  vec(0), vec(0), vec(0), vec(0),
                  low(lora[0]), low(lora[1]), low(lora[2]),
                  pl.BlockSpec((tc, tc), lambda i, j: (0, 0))],
        out_specs=(pl.BlockSpec((tm, tc), lambda i, j: (i, j)),) * 7,
        scratch_shapes=[pltpu.VMEM((tm, lora[0]), bf16), pltpu.VMEM((tm, lora[1]), bf16),
                        pltpu.VMEM((tm, lora[2]), bf16)],
        compiler_params=_params("parallel", "arbitrary"),
        name="rwkv_prep",
    )(pa_rkv, pa_rkv, pa_rkv, carried_rkv, carried_rkv, carried_rkv, pa_rkv, pa_rkv, pa_rkv,
      mu_rkv, mu_rkv, mu_rkv, pa_lora, carried_lora, pa_lora, mu_lora,
      w0, a0, k_k, k_a, w2, a2, g2, _head_indicator(tc))


def _bd(x, block):
    n = x.shape[1] // block
    lane_blk = lax.broadcasted_iota(jnp.int32, x.shape, 1) // block
    return jnp.concatenate([jnp.where(lane_blk == h, x, 0.0) for h in range(n)], axis=0)


def _diag_blocks(x, block):
    n = x.shape[1] // block
    rows = x.shape[0] // n
    lane_blk = lax.broadcasted_iota(jnp.int32, (rows, x.shape[1]), 1) // block
    out = x[0:rows]
    for h in range(1, n):
        out = jnp.where(lane_blk == h, x[h * rows:(h + 1) * rows], out)
    return out


def _mm(a, b, dims=_NN):
    return lax.dot_general(a.astype(bf16), b.astype(bf16), dims, preferred_element_type=f32)


def _wkv_local(ins, cls, masks, chunk):
    strict, incl, eye, eye_state = masks
    n = HEAD_SIZE
    gn = cls[0].shape[1]
    mw = (gn // n) * chunk
    cat = jnp.concatenate
    each = range(len(ins))

    pre = []
    for (r, lw, k, v, kk, a), cl in zip(ins, cls):
        cl_last = cl[chunk - 1:chunk, :]
        at = -kk * jnp.exp(cl - lw)
        rt = r * jnp.exp(cl)
        e_inv = jnp.exp(-cl)
        e_tail = jnp.exp(cl_last - cl)
        b = kk * a
        pre.append((at, rt, b * e_inv, k * e_inv, b * e_tail, k * e_tail, jnp.exp(cl_last)))

    d = [_mm(cat([at, rt], axis=0), cat([_bd(bt, n), _bd(kt, n)], axis=0), _NT)
         for at, rt, bt, kt, _, _, _ in pre]
    a_ab = [jnp.where(strict, x[:chunk, :mw], 0.0) for x in d]
    a_ak = [jnp.where(strict, x[:chunk, mw:], 0.0) for x in d]
    a_rr = [cat([jnp.where(incl, x[chunk:, :mw], 0.0), jnp.where(incl, x[chunk:, mw:], 0.0)], axis=1) for x in d]

    inv = [eye + x for x in a_ab]
    apow = a_ab
    apow_bd = [_bd(x, chunk).astype(bf16) for x in apow]
    for _ in range(chunk.bit_length() - 2):
        apow = [_mm(apow[i], apow_bd[i]) for i in each]
        apow_bd = [_bd(x, chunk).astype(bf16) for x in apow]
        inv = [inv[i] + _mm(inv[i], apow_bd[i]) for i in each]

    v_bd = [_bd(x[3], n).astype(bf16) for x in ins]
    rhs1 = [_mm(a_ak[i], v_bd[i]) for i in each]
    x = [_mm(inv[i], cat([_bd(rhs1[i], n), _bd(pre[i][0], n)], axis=1)) for i in each]
    u_loc = [z[:, :gn] for z in x]
    gh = [z[:, gn:] for z in x]
    yq = [_mm(a_rr[i], cat([cat([_bd(u_loc[i], n), _bd(gh[i], n)], axis=1).astype(bf16),
                            cat([v_bd[i], jnp.zeros_like(v_bd[i])], axis=1)], axis=0)) for i in each]
    ms = [_mm(cat([pre[i][4], pre[i][5]], axis=0),
              cat([cat([gh[i], u_loc[i]], axis=1), cat([jnp.zeros_like(ins[i][3]), ins[i][3]], axis=1)], axis=0), _TN)
          for i in each]
    y_loc = [z[:, :gn] for z in yq]
    q = [pre[i][1] + yq[i][:, gn:] for i in each]
    mt = [_diag_blocks(ms[i][:, :gn], n) + eye_state * pre[i][6] for i in each]
    sloc = [_diag_blocks(z[:, gn:], n) for z in ms]
    return y_loc, q, mt, sloc


def _wkv_kernel(r_ref, lw_ref, k_ref, v_ref, kk_ref, a_ref, g_ref, rk_ref, lng_ref, lnb_ref, s0_ref,
                o_ref, st_ref, state_ref, *, chunk):
    n = HEAD_SIZE
    nb, lt, width = r_ref.shape
    gn = (128 // chunk) * n
    tstep = pl.program_id(2)

    @pl.when(tstep == 0)
    def _():
        state_ref[...] = s0_ref[...]

    mrow = lax.broadcasted_iota(jnp.int32, (chunk, 128), 0)
    mcol = lax.broadcasted_iota(jnp.int32, (chunk, 128), 1) % chunk
    srow = lax.broadcasted_iota(jnp.int32, (n, gn), 0)
    scol = lax.broadcasted_iota(jnp.int32, (n, gn), 1) % n
    masks = (mrow > mcol, mrow >= mcol, (mrow == mcol).astype(f32), (srow == scol).astype(f32))
    trow = lax.broadcasted_iota(jnp.int32, (chunk, chunk), 0)
    tcol = lax.broadcasted_iota(jnp.int32, (chunk, chunk), 1)
    tri = (trow >= tcol).astype(bf16)
    irow = lax.broadcasted_iota(jnp.int32, (gn, gn), 0) // n
    icol = lax.broadcasted_iota(jnp.int32, (gn, gn), 1) // n
    head_ind = (irow == icol).astype(bf16)

    chains = [(bi, p) for bi in range(nb) for p in range(width // gn)]
    nchunk = lt // chunk
    probs = [(bi, p, ci) for bi, p in chains for ci in range(nchunk)]
    where = {pr: i for i, pr in enumerate(probs)}

    def index(bi, p, ci):
        return bi, slice(ci * chunk, (ci + 1) * chunk), slice(p * gn, (p + 1) * gn)

    ins = [tuple(ref[index(*pr)] for ref in (r_ref, lw_ref, k_ref, v_ref, kk_ref, a_ref)) for pr in probs]
    halves = [_split(x[1]) for x in ins]
    cls = [lax.dot_general(tri, hi, _NN, preferred_element_type=f32)
           + lax.dot_general(tri, lo, _NN, preferred_element_type=f32) for hi, lo in halves]
    y_loc, q, mt, sloc = _wkv_local(ins, cls, masks, chunk)

    st = {ch: state_ref[ch] for ch in chains}
    ys, stats = {}, {}
    for step in range(nchunk + 2):
        if step < nchunk:
            for ch in chains:
                i = where[ch + (step,)]
                mt_hi, mt_lo = _split(mt[i])
                r2 = lax.dot_general(jnp.concatenate([q[i].astype(bf16), mt_hi, mt_lo], axis=0),
                                     _bd(st[ch], n).astype(bf16), _NN, preferred_element_type=f32)
                ys[ch, step] = r2[:chunk] + y_loc[i]
                st[ch] = r2[chunk:chunk + n] + r2[chunk + n:] + sloc[i]
        if 0 <= step - 1 < nchunk:
            for ch in chains:
                idx = index(*ch, step - 1)
                y_hi, y_lo = _split(ys[ch, step - 1])
                bonus_w = (r_ref[idx] * k_ref[idx] * rk_ref[:, idx[2]]).astype(bf16)
                stats[ch, step - 1] = lax.dot_general(jnp.concatenate([y_hi, y_lo, bonus_w], axis=0), head_ind,
                                                      _NN, preferred_element_type=f32)
        if 0 <= step - 2 < nchunk:
            for ch in chains:
                idx = index(*ch, step - 2)
                lanes = idx[2]
                sm = stats.pop((ch, step - 2))
                dlt = ys.pop((ch, step - 2)) - (sm[:chunk] + sm[chunk:2 * chunk]) * (1.0 / n)
                var = lax.dot_general((dlt * dlt).astype(bf16), head_ind, _NN,
                                      preferred_element_type=f32) * (1.0 / n)
                out = (dlt * lax.rsqrt(var + GN_EPS) * lng_ref[:, lanes] + lnb_ref[:, lanes]
                       + sm[2 * chunk:] * v_ref[idx])
                o_ref[idx] = (out * g_ref[idx]).astype(o_ref.dtype)
    for ch in chains:
        state_ref[ch] = st[ch]

    @pl.when(tstep == pl.num_programs(2) - 1)
    def _():
        st_ref[...] = state_ref[...]


def _wkv(r, lw, k, v, kk, a, g, r_k, lnx_g, lnx_b, s0, chunk, rows_per_step, seq_per_step, lanes_per_step):
    bsz, t, c = r.shape
    n = HEAD_SIZE
    grp = 128 // chunk
    gn = grp * n
    heads = c // n
    nb, lt, width = seq_per_step, rows_per_step, lanes_per_step
    pg = width // gn
    s0k = s0.reshape(bsz, heads // grp, grp, n, n).transpose(0, 1, 4, 2, 3).reshape(bsz, heads // grp, n, gn)
    tok = pl.BlockSpec((nb, lt, width), lambda b, gi, ti: (b, ti, gi))
    vec = pl.BlockSpec((1, width), lambda b, gi, ti: (0, gi))
    st = pl.BlockSpec((nb, pg, n, gn), lambda b, gi, ti: (b, gi, 0, 0))
    out, stk = pl.pallas_call(
        functools.partial(_wkv_kernel, chunk=chunk),
        out_shape=(jax.ShapeDtypeStruct((bsz, t, c), bf16), jax.ShapeDtypeStruct(s0k.shape, f32)),
        grid=(bsz // nb, c // width, t // lt),
        in_specs=[tok] * 7 + [vec] * 3 + [st],
        out_specs=(tok, st),
        scratch_shapes=[pltpu.VMEM((nb, pg, n, gn), f32)],
        compiler_params=_params("parallel", "parallel", "arbitrary"),
        name="wkv",
    )(r, lw, k, v, kk, a, g, r_k, lnx_g, lnx_b, s0k)
    s_new = stk.reshape(bsz, heads // grp, n, grp, n).transpose(0, 1, 3, 4, 2).reshape(bsz, heads, n, n)
    return out, s_new


def _conv_kernel(u_ref, halo_ref, st_ref, w_ref, b_ref, o_ref, ext_ref, *, taps, sub_rows):
    nb, tt, _ = u_ref.shape
    first = pl.program_id(1) == 0

    @pl.when(first)
    def _():
        ext_ref[:, 0:CONV_HALO, :] = st_ref[...]

    @pl.when(jnp.logical_not(first))
    def _():
        ext_ref[:, 0:CONV_HALO, :] = halo_ref[...]

    ext_ref[:, CONV_HALO:CONV_HALO + tt, :] = u_ref[...]
    lead = CONV_HALO - (taps - 1)
    for bi in range(nb):
        for r0 in range(0, tt, sub_rows):
            acc = None
            for b in range(8):
                rows = sub_rows + (8 if b else 0)
                z = None
                for a8 in range(0, CONV_HALO + 8, 8):
                    tap = a8 + b - lead
                    if 0 <= tap < taps:
                        term = w_ref[tap:tap + 1, :] * ext_ref[bi, r0 + a8:r0 + a8 + rows, :]
                        z = term if z is None else z + term
                if z is not None:
                    z = z[b:b + sub_rows] if b else z
                    acc = z if acc is None else acc + z
            o_ref[bi, r0:r0 + sub_rows, :] = acc + b_ref[...]


def _conv(u, hist, dw_w, dw_b, seq_per_tile, rows_per_tile):
    bsz, t, c = u.shape
    nb, tt = seq_per_tile, rows_per_tile
    tc = _tile(c, 128)
    hpt = tt // CONV_HALO
    taps = dw_w.shape[0]
    return pl.pallas_call(
        functools.partial(_conv_kernel, taps=taps, sub_rows=_tile(tt, 128)),
        out_shape=jax.ShapeDtypeStruct((bsz, t, c), f32),
        grid=(bsz // nb, t // tt, c // tc),
        in_specs=[pl.BlockSpec((nb, tt, tc), lambda b, i, j: (b, i, j)),
                  pl.BlockSpec((nb, CONV_HALO, tc), lambda b, i, j: (b, jnp.maximum(i * hpt - 1, 0), j)),
                  pl.BlockSpec((nb, CONV_HALO, tc), lambda b, i, j: (b, 0, j)),
                  pl.BlockSpec((taps, tc), lambda b, i, j: (0, j)),
                  pl.BlockSpec((1, tc), lambda b, i, j: (0, j))],
        out_specs=pl.BlockSpec((nb, tt, tc), lambda b, i, j: (b, i, j)),
        scratch_shapes=[pltpu.VMEM((nb, CONV_HALO + tt, tc), f32)],
        compiler_params=_params("parallel", "arbitrary", "arbitrary"),
        name="dwconv",
    )(u, u, hist, dw_w, dw_b)


def _ln_silu_kernel(x_ref, g_ref, b_ref, o_ref):
    x = x_ref[...]
    mu = jnp.mean(x, axis=-1, keepdims=True)
    d = x - mu
    var = jnp.mean(d * d, axis=-1, keepdims=True)
    y = d * lax.rsqrt(var + LN_EPS) * g_ref[...] + b_ref[...]
    o_ref[...] = (y * _sigmoid(y)).astype(o_ref.dtype)


def _ln_silu(x, g, b):
    m, d = x.shape
    tm = _tile(m, 512)
    return pl.pallas_call(
        _ln_silu_kernel,
        out_shape=jax.ShapeDtypeStruct((m, d), bf16),
        grid=(m // tm,),
        in_specs=[pl.BlockSpec((tm, d), lambda i: (i, 0)), pl.BlockSpec((1, d), lambda i: (0, 0)),
                  pl.BlockSpec((1, d), lambda i: (0, 0))],
        out_specs=pl.BlockSpec((tm, d), lambda i: (i, 0)),
        compiler_params=_params("parallel"),
        name="ln_silu",
    )(x, g, b)


def _layer(x, p, s_wkv, s_shift, s_conv, w, *, chunk):
    bsz, t, d = x.shape
    m = bsz * t
    c = w["w_o_a"].shape[0]
    c3 = 3 * c
    n_shift = s_shift.shape[1]
    n_lora = n_shift - c3
    x2 = x.reshape(m, d)

    h = _rms_cast(x2, w["g_pre_mix"])
    lw_pad = -(-n_lora // 128) * 128
    pa_rkv = _matmul(h, w["w_in"], c3, tm_pref=1024, tn_pref=512, name="mm_rkv")
    pa_lora = _matmul(h, w["w_in"], lw_pad, col0=c3, tn_pref=lw_pad, name="mm_lora")
    u = _matmul_glu(h, w["w_in_cg"], c)
    gates = _matmul(h, w["w_in_cg"], 2 * d, col0=2 * c, act="sigmoid", name="mm_gate")

    carried_lora = jnp.pad(s_shift[:, c3:], ((0, 0), (0, lw_pad - n_lora)))
    prep = _prep(pa_rkv, s_shift[:, None, :c3], pa_lora, carried_lora[:, None, :],
                 w["mu_rkv"], w["mu_lora"], w["w0"], w["a0"], w["k_k"], w["k_a"], w["w2"], w["a2"], w["g2"], t)
    r, lw, k, v, kk, a, g = (z.reshape(bsz, t, c) for z in prep)
    wkv_rows = _tile(t, 256)
    wkv_lanes = max(256, (128 // chunk) * HEAD_SIZE)
    problems = (wkv_rows // chunk) * (wkv_lanes // ((128 // chunk) * HEAD_SIZE))
    ya_in, s_new = _wkv(r, lw, k, v, kk, a, g, w["r_k"], w["lnx_g"], w["lnx_b"], s_wkv, chunk=chunk,
                        rows_per_step=wkv_rows, seq_per_step=_tile(bsz, max(1, 8 // problems)),
                        lanes_per_step=wkv_lanes)

    last_rows = lambda z, width: lax.slice(z, (t - 1, 0), (m, width), (t, 1))
    shift_new = jnp.concatenate([last_rows(pa_rkv, c3), last_rows(pa_lora, n_lora)], axis=1)

    u3 = u.reshape(bsz, t, c)
    hist = jnp.pad(s_conv, ((0, 0), (CONV_HALO - s_conv.shape[1], 0), (0, 0)))
    conv_rows = _tile(t, 512)
    cv = _conv(u3, hist, w["dw_w"], w["dw_b"], _tile(bsz, max(1, 512 // conv_rows)), conv_rows)
    cv = _ln_silu(cv.reshape(m, c), w["lnc_g"], w["lnc_b"])
    keep = s_conv.shape[1]
    conv_new = jnp.concatenate([s_conv, u3], axis=1)[:, -keep:] if t < keep else u3[:, t - keep:]

    merged = _matmul_merge(ya_in.reshape(m, c), cv, w["w_o_a"], w["w_o_b"], gates)
    x1, hf = _out_proj(merged, w["w_out"], x2, w["g_post_mix"], w["g_pre_ffn"])
    x2_, hp = _ffn(hf, w["w_up"], w["w_down"], x1, w["g_post_ffn"], w["g_pre_ple"])
    y = _ple(hp, w["w_ple_gate"], p.reshape(m, p.shape[-1]), w["w_ple"], x2_, w["g_post_ple"])
    return y.reshape(bsz, t, d), s_new, shift_new, conv_new


def _layer_weights(i, c, g_pre_mix, w_in, mu_shift, w0, w2, a0, a2, g2, k_k, k_a, r_k, lnx_g, lnx_b, w_o_a,
                   dw_w, dw_b, lnc_g, lnc_b, w_o_b, w_out, g_post_mix, g_pre_ffn, w_up, w_down, g_post_ffn,
                   g_pre_ple, w_ple_gate, w_ple, g_post_ple):
    c3 = 3 * c
    n_lora = w2.shape[1] + a2.shape[1] + g2.shape[1]
    n_shift = c3 + n_lora
    lw_pad = -(-n_lora // 128) * 128
    row = lambda z: z[i].reshape(1, -1)
    win = w_in[i]
    return {
        "g_pre_mix": row(g_pre_mix),
        "w_in": win,
        "w_in_cg": win[:, n_shift:].astype(bf16),
        "mu_rkv": mu_shift[i, :c3].reshape(1, -1),
        "mu_lora": jnp.pad(mu_shift[i, c3:], (0, lw_pad - n_lora)).reshape(1, -1),
        "w0": row(w0), "a0": row(a0), "k_k": row(k_k), "k_a": row(k_a),
        "w2": w2[i].astype(bf16), "a2": a2[i].astype(bf16), "g2": g2[i].astype(bf16),
        "r_k": row(r_k), "lnx_g": row(lnx_g), "lnx_b": row(lnx_b),
        "w_o_a": w_o_a[i].astype(bf16), "w_o_b": w_o_b[i].astype(bf16), "w_out": w_out[i].astype(bf16),
        "dw_w": dw_w[i], "dw_b": row(dw_b), "lnc_g": row(lnc_g), "lnc_b": row(lnc_b),
        "g_post_mix": row(g_post_mix), "g_pre_ffn": row(g_pre_ffn),
        "w_up": w_up[i].astype(bf16), "w_down": w_down[i].astype(bf16),
        "g_post_ffn": row(g_post_ffn), "g_pre_ple": row(g_pre_ple),
        "w_ple_gate": w_ple_gate[i].astype(bf16), "w_ple": w_ple[i].astype(bf16),
        "g_post_ple": row(g_post_ple),
    }


def kernel(x_prompt, x_sample, p_prompt, p_sample, state_wkv, state_shift, state_conv, g_pre_mix, w_in, mu_shift, w0, w2, a0, a2, g2, k_k, k_a, r_k, lnx_g, lnx_b, w_o_a, dw_w, dw_b, lnc_g, lnc_b, w_o_b, w_out, g_post_mix, g_pre_ffn, w_up, w_down, g_post_ffn, g_pre_ple, w_ple_gate, w_ple, g_post_ple):
    depth = w_in.shape[0]
    c = w_o_a.shape[1]
    heads = c // HEAD_SIZE
    bp, tp, _ = x_prompt.shape
    ts = x_sample.shape[1]
    xp, xs = x_prompt, x_sample
    outs = [[] for _ in range(6)]
    for i in range(depth):
        w = _layer_weights(i, c, g_pre_mix, w_in, mu_shift, w0, w2, a0, a2, g2, k_k, k_a, r_k, lnx_g, lnx_b,
                           w_o_a, dw_w, dw_b, lnc_g, lnc_b, w_o_b, w_out, g_post_mix, g_pre_ffn, w_up, w_down,
                           g_post_ffn, g_pre_ple, w_ple_gate, w_ple, g_post_ple)
        xp, s1, s2, s3 = _layer(xp, p_prompt[i],
                                jnp.zeros((bp, heads, HEAD_SIZE, HEAD_SIZE), f32),
                                jnp.zeros((bp, state_shift.shape[2]), f32),
                                jnp.zeros((bp,) + state_conv.shape[2:], f32), w,
                                chunk=min(64, tp))
        xs, t1, t2, t3 = _layer(xs, p_sample[i], state_wkv[i], state_shift[i], state_conv[i], w,
                                chunk=min(64, ts))
        for lst, val in zip(outs, (s1, s2, s3, t1, t2, t3)):
            lst.append(val)
    return (xp, xs) + tuple(jnp.stack(o) for o in outs)
```

```python
import functools

import jax
import jax.numpy as jnp
from jax import lax
from jax.experimental import pallas as pl
from jax.experimental.pallas import tpu as pltpu

f32 = jnp.float32
bf16 = jnp.bfloat16

HEAD_SIZE = 64
CONV_K = 31
CONV_HALO = 32
RMS_EPS = 1e-6
LN_EPS = 1e-5
GN_EPS = 64e-5
KK_EPS = 1e-12
VMEM_LIMIT = 56 * 1024 * 1024
SUBLANES = 8
WKV_PROBLEMS = 16

_NN = (((1,), (0,)), ((), ()))
_NT = (((1,), (1,)), ((), ()))
_TN = (((0,), (0,)), ((), ()))


def _params(*sem):
    return pltpu.CompilerParams(dimension_semantics=sem, vmem_limit_bytes=VMEM_LIMIT)


def _tile(n, pref):
    t = min(n, pref)
    while n % t:
        t -= 1
    return t


def _col_tiles(w, tn=512):
    k, n = w.shape
    tn = _tile(n, tn)
    return w.reshape(k, n // tn, tn).transpose(1, 0, 2)


def _dot(a, b, dims=_NN):
    return lax.dot_general(a.astype(bf16), b.astype(bf16), dims, preferred_element_type=f32)


def _split(a):
    hi = a.astype(bf16)
    lo = (a - hi.astype(f32)).astype(bf16)
    return hi, lo


def _dot3(a, b, dims=_NN):
    a_hi, a_lo = _split(a)
    b_hi, b_lo = _split(b)
    d = functools.partial(lax.dot_general, dimension_numbers=dims, preferred_element_type=f32)
    return d(a_hi, b_hi) + (d(a_hi, b_lo) + d(a_lo, b_hi))


def _dot_exact_rhs(a, b_bf16, dims=_NN):
    a_hi, a_lo = _split(a)
    d = functools.partial(lax.dot_general, dimension_numbers=dims, preferred_element_type=f32)
    return d(a_hi, b_bf16) + d(a_lo, b_bf16)


def _sigmoid(x):
    return 1.0 / (1.0 + jnp.exp(-x))


def _rms(x, g):
    return x * lax.rsqrt(jnp.mean(x * x, axis=-1, keepdims=True) + RMS_EPS) * g


def _rms_cast_kernel(x_ref, g_ref, o_ref):
    o_ref[...] = _rms(x_ref[...], g_ref[...]).astype(o_ref.dtype)


def _rms_cast(x, g):
    m, d = x.shape
    tm = _tile(m, 512)
    return pl.pallas_call(
        _rms_cast_kernel,
        out_shape=jax.ShapeDtypeStruct((m, d), bf16),
        grid=(m // tm,),
        in_specs=[pl.BlockSpec((tm, d), lambda i: (i, 0)), pl.BlockSpec((1, d), lambda i: (0, 0))],
        out_specs=pl.BlockSpec((tm, d), lambda i: (i, 0)),
        compiler_params=_params("parallel"),
        name="rms_cast",
    )(x, g)


def _dot_wt(a, wt_ref):
    return lax.dot_general(a, wt_ref[...].astype(bf16), _NT, preferred_element_type=f32)


def _wt_spec(tn, k, row0):
    assert row0 % SUBLANES == 0 and tn % SUBLANES == 0
    return pl.BlockSpec((pl.Element(tn), pl.Element(k)),
                        lambda i, j: (pl.multiple_of(row0 + j * tn, SUBLANES), 0))


def _mm_kernel(a_ref, wt_ref, o_ref, *, act):
    acc = _dot_wt(a_ref[...], wt_ref)
    if act == "sigmoid":
        acc = _sigmoid(acc)
    o_ref[...] = acc.astype(o_ref.dtype)


def _matmul(a, wt, n, row0=0, act=None, out_dtype=f32, tm_pref=1024, tn_pref=512, name="mm"):
    m, k = a.shape
    tm, tn = _tile(m, tm_pref), _tile(n, tn_pref)
    return pl.pallas_call(
        functools.partial(_mm_kernel, act=act),
        out_shape=jax.ShapeDtypeStruct((m, n), out_dtype),
        grid=(m // tm, n // tn),
        in_specs=[pl.BlockSpec((tm, k), lambda i, j: (i, 0)), _wt_spec(tn, k, row0)],
        out_specs=pl.BlockSpec((tm, tn), lambda i, j: (i, j)),
        compiler_params=_params("parallel", "arbitrary"),
        name=name,
    )(a, wt)


def _glu_kernel(a_ref, w1_ref, w2_ref, o_ref):
    a = a_ref[...]
    o_ref[...] = _dot_wt(a, w1_ref) * _sigmoid(_dot_wt(a, w2_ref))


def _matmul_glu(a, wt, n, row0):
    m, k = a.shape
    tm, tn = _tile(m, 1024), _tile(n, 256)
    return pl.pallas_call(
        _glu_kernel,
        out_shape=jax.ShapeDtypeStruct((m, n), f32),
        grid=(m // tm, n // tn),
        in_specs=[pl.BlockSpec((tm, k), lambda i, j: (i, 0)), _wt_spec(tn, k, row0), _wt_spec(tn, k, row0 + n)],
        out_specs=pl.BlockSpec((tm, tn), lambda i, j: (i, j)),
        compiler_params=_params("parallel", "arbitrary"),
        name="mm_glu",
    )(a, wt, wt)


def _merge_kernel(a1_ref, a2_ref, b1_ref, b2_ref, ga_ref, gb_ref, o_ref):
    ya = jnp.dot(a1_ref[...], b1_ref[...], preferred_element_type=f32)
    yb = jnp.dot(a2_ref[...], b2_ref[...], preferred_element_type=f32)
    o_ref[...] = (ga_ref[...] * ya + gb_ref[...] * yb).astype(o_ref.dtype)


def _matmul_merge(a1, a2, b1, b2, gates):
    m, k = a1.shape
    nj, _, tn = b1.shape
    n = nj * tn
    tm = _tile(m, 512)
    return pl.pallas_call(
        _merge_kernel,
        out_shape=jax.ShapeDtypeStruct((m, n), bf16),
        grid=(m // tm, nj),
        in_specs=[pl.BlockSpec((tm, k), lambda i, j: (i, 0)),
                  pl.BlockSpec((tm, k), lambda i, j: (i, 0)),
                  pl.BlockSpec((None, k, tn), lambda i, j: (j, 0, 0)),
                  pl.BlockSpec((None, k, tn), lambda i, j: (j, 0, 0)),
                  pl.BlockSpec((tm, tn), lambda i, j: (i, j)),
                  pl.BlockSpec((tm, tn), lambda i, j: (i, j + nj))],
        out_specs=pl.BlockSpec((tm, tn), lambda i, j: (i, j)),
        compiler_params=_params("parallel", "arbitrary"),
        name="mm_merge",
    )(a1, a2, b1, b2, gates, gates)


ROW_BLOCK = 256


def _row_blocks(ref):
    sub = _tile(ref.shape[0], ROW_BLOCK)
    return [slice(r, r + sub) for r in range(0, ref.shape[0], sub)]


def _accumulate(o_ref, step, part):
    @pl.when(step == 0)
    def _():
        for rows in _row_blocks(o_ref):
            o_ref[rows, :] = part(rows)

    @pl.when(step > 0)
    def _():
        for rows in _row_blocks(o_ref):
            o_ref[rows, :] += part(rows)


_ONCE = pl.Buffered(1)


def _out_proj_kernel(a_ref, b_ref, x_ref, g1_ref, g2_ref, xo_ref, ho_ref):
    kstep = pl.program_id(1)
    _accumulate(xo_ref, kstep, lambda rows: jnp.dot(a_ref[rows, :], b_ref[...], preferred_element_type=f32))

    @pl.when(kstep == pl.num_programs(1) - 1)
    def _():
        for rows in _row_blocks(xo_ref):
            x1 = x_ref[rows, :] + _rms(xo_ref[rows, :], g1_ref[...])
            xo_ref[rows, :] = x1
            ho_ref[rows, :] = _rms(x1, g2_ref[...]).astype(ho_ref.dtype)


def _out_proj(a, b, x, g_post, g_next):
    m, k = a.shape
    n = b.shape[1]
    tm, tk = _tile(m, 512), _tile(k, 512)
    return pl.pallas_call(
        _out_proj_kernel,
        out_shape=(jax.ShapeDtypeStruct((m, n), f32), jax.ShapeDtypeStruct((m, n), bf16)),
        grid=(m // tm, k // tk),
        in_specs=[pl.BlockSpec((tm, tk), lambda i, kk: (i, kk)),
                  pl.BlockSpec((tk, n), lambda i, kk: (kk, 0)),
                  pl.BlockSpec((tm, n), lambda i, kk: (i, 0), pipeline_mode=_ONCE),
                  pl.BlockSpec((1, n), lambda i, kk: (0, 0)),
                  pl.BlockSpec((1, n), lambda i, kk: (0, 0))],
        out_specs=(pl.BlockSpec((tm, n), lambda i, kk: (i, 0)),
                   pl.BlockSpec((tm, n), lambda i, kk: (i, 0))),
        compiler_params=_params("parallel", "arbitrary"),
        name="out_proj",
    )(a, b, x, g_post, g_next)


def _ffn_kernel(h_ref, wu_ref, wd_ref, x_ref, g1_ref, g2_ref, xo_ref, ho_ref):
    fstep = pl.program_id(1)

    def part(rows):
        up = jnp.dot(h_ref[rows, :], wu_ref[...], preferred_element_type=f32)
        act = jnp.square(jnp.maximum(up, 0.0)).astype(bf16)
        return jnp.dot(act, wd_ref[...], preferred_element_type=f32)

    _accumulate(xo_ref, fstep, part)

    @pl.when(fstep == pl.num_programs(1) - 1)
    def _():
        for rows in _row_blocks(xo_ref):
            x2 = x_ref[rows, :] + _rms(xo_ref[rows, :], g1_ref[...])
            xo_ref[rows, :] = x2
            ho_ref[rows, :] = _rms(x2, g2_ref[...]).astype(ho_ref.dtype)


def _ffn(h, w_up, w_down, x, g_post, g_next):
    m, d = h.shape
    nf, _, tf = w_up.shape
    tm = _tile(m, 512)
    return pl.pallas_call(
        _ffn_kernel,
        out_shape=(jax.ShapeDtypeStruct((m, d), f32), jax.ShapeDtypeStruct((m, d), bf16)),
        grid=(m // tm, nf),
        in_specs=[pl.BlockSpec((tm, d), lambda i, j: (i, 0), pipeline_mode=_ONCE),
                  pl.BlockSpec((None, d, tf), lambda i, j: (j, 0, 0)),
                  pl.BlockSpec((tf, d), lambda i, j: (j, 0)),
                  pl.BlockSpec((tm, d), lambda i, j: (i, 0), pipeline_mode=_ONCE),
                  pl.BlockSpec((1, d), lambda i, j: (0, 0)),
                  pl.BlockSpec((1, d), lambda i, j: (0, 0))],
        out_specs=(pl.BlockSpec((tm, d), lambda i, j: (i, 0)),
                   pl.BlockSpec((tm, d), lambda i, j: (i, 0))),
        compiler_params=_params("parallel", "arbitrary"),
        name="ffn",
    )(h, w_up, w_down, x, g_post, g_next)


def _ple_kernel(h_ref, wg_ref, p_ref, wp_ref, x_ref, g_ref, o_ref):
    kstep = pl.program_id(1)
    _accumulate(o_ref, kstep, lambda rows: jnp.dot(h_ref[rows, :], wg_ref[...], preferred_element_type=f32))

    @pl.when(kstep == pl.num_programs(1) - 1)
    def _():
        for rows in _row_blocks(o_ref):
            pe = jnp.dot(p_ref[rows, :].astype(bf16), wp_ref[...], preferred_element_type=f32)
            o_ref[rows, :] = x_ref[rows, :] + _rms(pe * _sigmoid(o_ref[rows, :]), g_ref[...])


def _ple(h, w_gate, p, w_ple, x, g_post):
    m, k = h.shape
    n = w_gate.shape[1]
    q = p.shape[1]
    tm, tk = _tile(m, 512), _tile(k, 512)
    return pl.pallas_call(
        _ple_kernel,
        out_shape=jax.ShapeDtypeStruct((m, n), f32),
        grid=(m // tm, k // tk),
        in_specs=[pl.BlockSpec((tm, tk), lambda i, kk: (i, kk)),
                  pl.BlockSpec((tk, n), lambda i, kk: (kk, 0)),
                  pl.BlockSpec((tm, q), lambda i, kk: (i, 0)),
                  pl.BlockSpec((q, n), lambda i, kk: (0, 0), pipeline_mode=_ONCE),
                  pl.BlockSpec((tm, n), lambda i, kk: (i, 0), pipeline_mode=_ONCE),
                  pl.BlockSpec((1, n), lambda i, kk: (0, 0))],
        out_specs=pl.BlockSpec((tm, n), lambda i, kk: (i, 0)),
        compiler_params=_params("parallel", "arbitrary"),
        name="ple",
    )(h, w_gate, p, w_ple, x, g_post)


def _shift(x_ref, carried_ref, above_ref, mu_ref, seg_len, starts_sequence):
    x = x_ref[...]
    rows, w = x.shape
    prev = pltpu.roll(x, 1, 0)
    first = (lax.broadcasted_iota(jnp.int32, (rows, w), 0) % seg_len) == 0
    nseq = carried_ref.shape[0]
    if nseq == 1:
        above = above_ref[above_ref.shape[0] - 1:, :]
        bnd_rows = jnp.broadcast_to(jnp.where(starts_sequence, carried_ref[0], above), (rows, w))
    else:
        bnd_rows = jnp.broadcast_to(carried_ref[...], (nseq, seg_len, w)).reshape(rows, w)
    prev = jnp.where(first, bnd_rows, prev)
    return x + (prev - x) * mu_ref[...]


def _prep_kernel(pr_ref, pk_ref, pv_ref, cr_ref, ck_ref, cv_ref, ar_ref, ak_ref, av_ref, mur_ref, muk_ref, muv_ref,
                 pl_ref, cl_ref, al8_ref, mul_ref, w0_ref, a0_ref, kkw_ref, kaw_ref,
                 w2_ref, a2_ref, g2_ref, e_ref,
                 r_ref, lw_ref, k_ref, v_ref, kk_ref, a_ref, g_ref,
                 tw_ref, al_ref, sg_ref, *, seg_len, tiles_per_seq, lora):
    dl, al_n, gl_n = lora
    starts = (pl.program_id(0) % tiles_per_seq) == 0

    @pl.when(pl.program_id(1) == 0)
    def _():
        xl = _shift(pl_ref, cl_ref, al8_ref, mul_ref, seg_len, starts)
        tw_ref[...] = jnp.tanh(xl[:, :dl]).astype(bf16)
        al_ref[...] = xl[:, dl:dl + al_n].astype(bf16)
        sg_ref[...] = _sigmoid(xl[:, dl + al_n:dl + al_n + gl_n]).astype(bf16)

    r = _shift(pr_ref, cr_ref, ar_ref, mur_ref, seg_len, starts)
    k = _shift(pk_ref, ck_ref, ak_ref, muk_ref, seg_len, starts)
    v = _shift(pv_ref, cv_ref, av_ref, muv_ref, seg_len, starts)

    dec = w0_ref[...] + jnp.dot(tw_ref[...], w2_ref[...], preferred_element_type=f32)
    z = -dec
    softplus = jnp.maximum(z, 0.0) + jnp.log(1.0 + jnp.exp(-jnp.abs(z)))
    lw = -jnp.exp(-softplus - 0.5)
    a = _sigmoid(a0_ref[...] + jnp.dot(al_ref[...], a2_ref[...], preferred_element_type=f32))
    g = jnp.dot(sg_ref[...], g2_ref[...], preferred_element_type=f32)

    kk = k * kkw_ref[...]
    ssq = _dot_exact_rhs(kk * kk, e_ref[...])
    kk = kk / jnp.maximum(jnp.sqrt(ssq), KK_EPS)
    k = k * (1.0 + (a - 1.0) * kaw_ref[...])

    r_ref[...] = r
    lw_ref[...] = lw
    k_ref[...] = k
    v_ref[...] = v
    kk_ref[...] = kk
    a_ref[...] = a
    g_ref[...] = g


def _head_indicator(width):
    h = jnp.arange(width) // HEAD_SIZE
    return (h[:, None] == h[None, :]).astype(bf16)


def _prep(pa_rkv, carried_rkv, pa_lora, carried_lora, mu_rkv, mu_lora, w0, a0, k_k, k_a, w2, a2, g2, t):
    m, c3 = pa_rkv.shape
    c = c3 // 3
    wl = pa_lora.shape[1]
    lora = (w2.shape[0], a2.shape[0], g2.shape[0])
    tm = _tile(m, 256)
    seg_len = min(t, tm)
    nseq, tiles_per_seq = tm // seg_len, t // seg_len
    tc = _tile(c, 512)
    nj = c // tc
    sub = 8
    above_row = lambda i: jnp.maximum(i * (tm // sub) - 1, 0)
    tok = lambda off: pl.BlockSpec((tm, tc), lambda i, j: (i, j + off))
    car = lambda off: pl.BlockSpec((nseq, 1, tc), lambda i, j: (i // tiles_per_seq, 0, j + off))
    abv = lambda off: pl.BlockSpec((sub, tc), lambda i, j: (above_row(i), j + off))
    vec = lambda off: pl.BlockSpec((1, tc), lambda i, j: (0, j + off))
    low = lambda n: pl.BlockSpec((n, tc), lambda i, j: (0, j))
    out = jax.ShapeDtypeStruct((m, c), f32)
    return pl.pallas_call(
        functools.partial(_prep_kernel, seg_len=seg_len, tiles_per_seq=tiles_per_seq, lora=lora),
        out_shape=(out,) * 7,
        grid=(m // tm, nj),
        in_specs=[tok(0), tok(nj), tok(2 * nj), car(0), car(nj), car(2 * nj), abv(0), abv(nj), abv(2 * nj),
                  vec(0), vec(nj), vec(2 * nj),
                  pl.BlockSpec((tm, wl), lambda i, j: (i, 0)),
                  pl.BlockSpec((nseq, 1, wl), lambda i, j: (i // tiles_per_seq, 0, 0)),
                  pl.BlockSpec((sub, wl), lambda i, j: (above_row(i), 0)),
                  pl.BlockSpec((1, wl), lambda i, j: (0, 0)),
                  vec(0), vec(0), vec(0), vec(0),
                  low(lora[0]), low(lora[1]), low(lora[2]),
                  pl.BlockSpec((tc, tc), lambda i, j: (0, 0))],
        out_specs=(pl.BlockSpec((tm, tc), lambda i, j: (i, j)),) * 7,
        scratch_shapes=[pltpu.VMEM((tm, lora[0]), bf16), pltpu.VMEM((tm, lora[1]), bf16),
                        pltpu.VMEM((tm, lora[2]), bf16)],
        compiler_params=_params("parallel", "arbitrary"),
        name="rwkv_prep",
    )(pa_rkv, pa_rkv, pa_rkv, carried_rkv, carried_rkv, carried_rkv, pa_rkv, pa_rkv, pa_rkv,
      mu_rkv, mu_rkv, mu_rkv, pa_lora, carried_lora, pa_lora, mu_lora,
      w0, a0, k_k, k_a, w2, a2, g2, _head_indicator(tc))


def _bd(x, block):
    n = x.shape[1] // block
    lane_blk = lax.broadcasted_iota(jnp.int32, x.shape, 1) // block
    return jnp.concatenate([jnp.where(lane_blk == h, x, 0.0) for h in range(n)], axis=0)


def _diag_blocks(x, block):
    n = x.shape[1] // block
    rows = x.shape[0] // n
    lane_blk = lax.broadcasted_iota(jnp.int32, (rows, x.shape[1]), 1) // block
    out = x[0:rows]
    for h in range(1, n):
        out = jnp.where(lane_blk == h, x[h * rows:(h + 1) * rows], out)
    return out


def _mm(a, b, dims=_NN):
    return lax.dot_general(a.astype(bf16), b.astype(bf16), dims, preferred_element_type=f32)


def _wkv_local(ins, cls, masks, chunk):
    strict, incl, eye, eye_state = masks
    n = HEAD_SIZE
    gn = cls[0].shape[1]
    mw = (gn // n) * chunk
    cat = jnp.concatenate
    each = range(len(ins))

    pre = []
    for (r, lw, k, v, kk, a), cl in zip(ins, cls):
        cl_last = cl[chunk - 1:chunk, :]
        at = -kk * jnp.exp(cl - lw)
        rt = r * jnp.exp(cl)
        e_inv = jnp.exp(-cl)
        e_tail = jnp.exp(cl_last - cl)
        b = kk * a
        pre.append((at, rt, b * e_inv, k * e_inv, b * e_tail, k * e_tail, jnp.exp(cl_last)))

    d = [_mm(cat([at, rt], axis=0), cat([_bd(bt, n), _bd(kt, n)], axis=0), _NT)
         for at, rt, bt, kt, _, _, _ in pre]
    a_ab = [jnp.where(strict, x[:chunk, :mw], 0.0) for x in d]
    a_ak = [jnp.where(strict, x[:chunk, mw:], 0.0) for x in d]
    a_rr = [cat([jnp.where(incl, x[chunk:, :mw], 0.0), jnp.where(incl, x[chunk:, mw:], 0.0)], axis=1) for x in d]

    inv = [eye + x for x in a_ab]
    apow = a_ab
    apow_bd = [_bd(x, chunk).astype(bf16) for x in apow]
    for _ in range(chunk.bit_length() - 2):
        apow = [_mm(apow[i], apow_bd[i]) for i in each]
        apow_bd = [_bd(x, chunk).astype(bf16) for x in apow]
        inv = [inv[i] + _mm(inv[i], apow_bd[i]) for i in each]

    v_bd = [_bd(x[3], n).astype(bf16) for x in ins]
    rhs1 = [_mm(a_ak[i], v_bd[i]) for i in each]
    x = [_mm(inv[i], cat([_bd(rhs1[i], n), _bd(pre[i][0], n)], axis=1)) for i in each]
    u_loc = [z[:, :gn] for z in x]
    gh = [z[:, gn:] for z in x]
    yq = [_mm(a_rr[i], cat([cat([_bd(u_loc[i], n), _bd(gh[i], n)], axis=1).astype(bf16),
                            cat([v_bd[i], jnp.zeros_like(v_bd[i])], axis=1)], axis=0)) for i in each]
    ms = [_mm(cat([pre[i][4], pre[i][5]], axis=0),
              cat([cat([gh[i], u_loc[i]], axis=1), cat([jnp.zeros_like(ins[i][3]), ins[i][3]], axis=1)], axis=0), _TN)
          for i in each]
    y_loc = [z[:, :gn] for z in yq]
    q = [pre[i][1] + yq[i][:, gn:] for i in each]
    mt = [_diag_blocks(ms[i][:, :gn], n) + eye_state * pre[i][6] for i in each]
    sloc = [_diag_blocks(z[:, gn:], n) for z in ms]
    return y_loc, q, mt, sloc


def _wkv_kernel(r_ref, lw_ref, k_ref, v_ref, kk_ref, a_ref, g_ref, rk_ref, lng_ref, lnb_ref, s0_ref,
                o_ref, st_ref, state_ref, *, chunk):
    n = HEAD_SIZE
    nb, lt, width = r_ref.shape
    gn = (128 // chunk) * n
    tstep = pl.program_id(2)

    @pl.when(tstep == 0)
    def _():
        state_ref[...] = s0_ref[...]

    mrow = lax.broadcasted_iota(jnp.int32, (chunk, 128), 0)
    mcol = lax.broadcasted_iota(jnp.int32, (chunk, 128), 1) % chunk
    srow = lax.broadcasted_iota(jnp.int32, (n, gn), 0)
    scol = lax.broadcasted_iota(jnp.int32, (n, gn), 1) % n
    masks = (mrow > mcol, mrow >= mcol, (mrow == mcol).astype(f32), (srow == scol).astype(f32))
    trow = lax.broadcasted_iota(jnp.int32, (chunk, chunk), 0)
    tcol = lax.broadcasted_iota(jnp.int32, (chunk, chunk), 1)
    tri = (trow >= tcol).astype(bf16)
    irow = lax.broadcasted_iota(jnp.int32, (gn, gn), 0) // n
    icol = lax.broadcasted_iota(jnp.int32, (gn, gn), 1) // n
    head_ind = (irow == icol).astype(bf16)

    chains = [(bi, p) for bi in range(nb) for p in range(width // gn)]
    nchunk = lt // chunk
    probs = [(bi, p, ci) for bi, p in chains for ci in range(nchunk)]
    where = {pr: i for i, pr in enumerate(probs)}

    def index(bi, p, ci):
        return bi, slice(ci * chunk, (ci + 1) * chunk), slice(p * gn, (p + 1) * gn)

    ins = [tuple(ref[index(*pr)] for ref in (r_ref, lw_ref, k_ref, v_ref, kk_ref, a_ref)) for pr in probs]
    halves = [_split(x[1]) for x in ins]
    cls = [lax.dot_general(tri, hi, _NN, preferred_element_type=f32)
           + lax.dot_general(tri, lo, _NN, preferred_element_type=f32) for hi, lo in halves]
    y_loc, q, mt, sloc = _wkv_local(ins, cls, masks, chunk)

    st = {ch: state_ref[ch] for ch in chains}
    ys, stats = {}, {}
    for step in range(nchunk + 2):
        if step < nchunk:
            for ch in chains:
                i = where[ch + (step,)]
                mt_hi, mt_lo = _split(mt[i])
                r2 = lax.dot_general(jnp.concatenate([q[i].astype(bf16), mt_hi, mt_lo], axis=0),
                                     _bd(st[ch], n).astype(bf16), _NN, preferred_element_type=f32)
                ys[ch, step] = r2[:chunk] + y_loc[i]
                st[ch] = r2[chunk:chunk + n] + r2[chunk + n:] + sloc[i]
        if 0 <= step - 1 < nchunk:
            for ch in chains:
                idx = index(*ch, step - 1)
                y_hi, y_lo = _split(ys[ch, step - 1])
                bonus_w = (r_ref[idx] * k_ref[idx] * rk_ref[:, idx[2]]).astype(bf16)
                stats[ch, step - 1] = lax.dot_general(jnp.concatenate([y_hi, y_lo, bonus_w], axis=0), head_ind,
                                                      _NN, preferred_element_type=f32)
        if 0 <= step - 2 < nchunk:
            for ch in chains:
                idx = index(*ch, step - 2)
                lanes = idx[2]
                sm = stats.pop((ch, step - 2))
                dlt = ys.pop((ch, step - 2)) - (sm[:chunk] + sm[chunk:2 * chunk]) * (1.0 / n)
                var = lax.dot_general((dlt * dlt).astype(bf16), head_ind, _NN,
                                      preferred_element_type=f32) * (1.0 / n)
                out = (dlt * lax.rsqrt(var + GN_EPS) * lng_ref[:, lanes] + lnb_ref[:, lanes]
                       + sm[2 * chunk:] * v_ref[idx])
                o_ref[idx] = (out * g_ref[idx]).astype(o_ref.dtype)
    for ch in chains:
        state_ref[ch] = st[ch]

    @pl.when(tstep == pl.num_programs(2) - 1)
    def _():
        st_ref[...] = state_ref[...]


def _wkv(r, lw, k, v, kk, a, g, r_k, lnx_g, lnx_b, s0, chunk, rows_per_step, seq_per_step, lanes_per_step):
    bsz, t, c = r.shape
    n = HEAD_SIZE
    grp = 128 // chunk
    gn = grp * n
    heads = c // n
    nb, lt, width = seq_per_step, rows_per_step, lanes_per_step
    pg = width // gn
    s0k = s0.reshape(bsz, heads // grp, grp, n, n).transpose(0, 1, 4, 2, 3).reshape(bsz, heads // grp, n, gn)
    tok = pl.BlockSpec((nb, lt, width), lambda b, gi, ti: (b, ti, gi))
    vec = pl.BlockSpec((1, width), lambda b, gi, ti: (0, gi))
    st = pl.BlockSpec((nb, pg, n, gn), lambda b, gi, ti: (b, gi, 0, 0))
    out, stk = pl.pallas_call(
        functools.partial(_wkv_kernel, chunk=chunk),
        out_shape=(jax.ShapeDtypeStruct((bsz, t, c), bf16), jax.ShapeDtypeStruct(s0k.shape, f32)),
        grid=(bsz // nb, c // width, t // lt),
        in_specs=[tok] * 7 + [vec] * 3 + [st],
        out_specs=(tok, st),
        scratch_shapes=[pltpu.VMEM((nb, pg, n, gn), f32)],
        compiler_params=_params("parallel", "parallel", "arbitrary"),
        name="wkv",
    )(r, lw, k, v, kk, a, g, r_k, lnx_g, lnx_b, s0k)
    s_new = stk.reshape(bsz, heads // grp, n, grp, n).transpose(0, 1, 3, 4, 2).reshape(bsz, heads, n, n)
    return out, s_new


def _conv_kernel(u_ref, halo_ref, st_ref, w_ref, b_ref, o_ref, ext_ref, *, taps, sub_rows):
    nb, tt, _ = u_ref.shape
    first = pl.program_id(1) == 0

    @pl.when(first)
    def _():
        ext_ref[:, 0:CONV_HALO, :] = st_ref[...]

    @pl.when(jnp.logical_not(first))
    def _():
        ext_ref[:, 0:CONV_HALO, :] = halo_ref[...]

    ext_ref[:, CONV_HALO:CONV_HALO + tt, :] = u_ref[...]
    lead = CONV_HALO - (taps - 1)
    for bi in range(nb):
        for r0 in range(0, tt, sub_rows):
            acc = None
            for b in range(8):
                rows = sub_rows + (8 if b else 0)
                z = None
                for a8 in range(0, CONV_HALO + 8, 8):
                    tap = a8 + b - lead
                    if 0 <= tap < taps:
                        term = w_ref[tap:tap + 1, :] * ext_ref[bi, r0 + a8:r0 + a8 + rows, :]
                        z = term if z is None else z + term
                if z is not None:
                    z = z[b:b + sub_rows] if b else z
                    acc = z if acc is None else acc + z
            o_ref[bi, r0:r0 + sub_rows, :] = acc + b_ref[...]


def _conv(u, hist, dw_w, dw_b, seq_per_tile, rows_per_tile):
    bsz, t, c = u.shape
    nb, tt = seq_per_tile, rows_per_tile
    tc = _tile(c, 128)
    hpt = tt // CONV_HALO
    taps = dw_w.shape[0]
    return pl.pallas_call(
        functools.partial(_conv_kernel, taps=taps, sub_rows=_tile(tt, 128)),
        out_shape=jax.ShapeDtypeStruct((bsz, t, c), f32),
        grid=(bsz // nb, t // tt, c // tc),
        in_specs=[pl.BlockSpec((nb, tt, tc), lambda b, i, j: (b, i, j)),
                  pl.BlockSpec((nb, CONV_HALO, tc), lambda b, i, j: (b, jnp.maximum(i * hpt - 1, 0), j)),
                  pl.BlockSpec((nb, CONV_HALO, tc), lambda b, i, j: (b, 0, j)),
                  pl.BlockSpec((taps, tc), lambda b, i, j: (0, j)),
                  pl.BlockSpec((1, tc), lambda b, i, j: (0, j))],
        out_specs=pl.BlockSpec((nb, tt, tc), lambda b, i, j: (b, i, j)),
        scratch_shapes=[pltpu.VMEM((nb, CONV_HALO + tt, tc), f32)],
        compiler_params=_params("parallel", "arbitrary", "arbitrary"),
        name="dwconv",
    )(u, u, hist, dw_w, dw_b)


def _ln_silu_kernel(x_ref, g_ref, b_ref, o_ref):
    x = x_ref[...]
    mu = jnp.mean(x, axis=-1, keepdims=True)
    d = x - mu
    var = jnp.mean(d * d, axis=-1, keepdims=True)
    y = d * lax.rsqrt(var + LN_EPS) * g_ref[...] + b_ref[...]
    o_ref[...] = (y * _sigmoid(y)).astype(o_ref.dtype)


def _ln_silu(x, g, b):
    m, d = x.shape
    tm = _tile(m, 512)
    return pl.pallas_call(
        _ln_silu_kernel,
        out_shape=jax.ShapeDtypeStruct((m, d), bf16),
        grid=(m // tm,),
        in_specs=[pl.BlockSpec((tm, d), lambda i: (i, 0)), pl.BlockSpec((1, d), lambda i: (0, 0)),
                  pl.BlockSpec((1, d), lambda i: (0, 0))],
        out_specs=pl.BlockSpec((tm, d), lambda i: (i, 0)),
        compiler_params=_params("parallel"),
        name="ln_silu",
    )(x, g, b)


def _layer(x, p, s_wkv, s_shift, s_conv, w, *, chunk):
    bsz, t, d = x.shape
    m = bsz * t
    c = w["lnx_g"].shape[1]
    c3 = 3 * c
    n_shift = s_shift.shape[1]
    n_lora = n_shift - c3
    x2 = x.reshape(m, d)

    h = _rms_cast(x2, w["g_pre_mix"])
    lw_pad = -(-n_lora // 128) * 128
    wt = w["w_in_t"]
    pa_rkv = _matmul(h, wt, c3, name="mm_rkv")
    pa_lora = _matmul(h, wt, lw_pad, row0=c3, tm_pref=512, tn_pref=lw_pad, name="mm_lora")
    u = _matmul_glu(h, wt, c, row0=n_shift)
    gates = _matmul(h, wt, 2 * d, row0=n_shift + 2 * c, act="sigmoid", name="mm_gate")

    carried_lora = jnp.pad(s_shift[:, c3:], ((0, 0), (0, lw_pad - n_lora)))
    prep = _prep(pa_rkv, s_shift[:, None, :c3], pa_lora, carried_lora[:, None, :],
                 w["mu_rkv"], w["mu_lora"], w["w0"], w["a0"], w["k_k"], w["k_a"], w["w2"], w["a2"], w["g2"], t)
    r, lw, k, v, kk, a, g = (z.reshape(bsz, t, c) for z in prep)
    wkv_rows = _tile(t, 512)
    wkv_lanes = max(256, (128 // chunk) * HEAD_SIZE)
    problems = (wkv_rows // chunk) * (wkv_lanes // ((128 // chunk) * HEAD_SIZE))
    ya_in, s_new = _wkv(r, lw, k, v, kk, a, g, w["r_k"], w["lnx_g"], w["lnx_b"], s_wkv, chunk=chunk,
                        rows_per_step=wkv_rows, seq_per_step=_tile(bsz, max(1, WKV_PROBLEMS // problems)),
                        lanes_per_step=wkv_lanes)

    last_rows = lambda z, width: lax.slice(z, (t - 1, 0), (m, width), (t, 1))
    shift_new = jnp.concatenate([last_rows(pa_rkv, c3), last_rows(pa_lora, n_lora)], axis=1)

    u3 = u.reshape(bsz, t, c)
    hist = jnp.pad(s_conv, ((0, 0), (CONV_HALO - s_conv.shape[1], 0), (0, 0)))
    conv_rows = _tile(t, 512)
    cv = _conv(u3, hist, w["dw_w"], w["dw_b"], _tile(bsz, max(1, 512 // conv_rows)), conv_rows)
    cv = _ln_silu(cv.reshape(m, c), w["lnc_g"], w["lnc_b"])
    keep = s_conv.shape[1]
    conv_new = jnp.concatenate([s_conv, u3], axis=1)[:, -keep:] if t < keep else u3[:, t - keep:]

    merged = _matmul_merge(ya_in.reshape(m, c), cv, w["w_o_a"], w["w_o_b"], gates)
    x1, hf = _out_proj(merged, w["w_out"], x2, w["g_post_mix"], w["g_pre_ffn"])
    x2_, hp = _ffn(hf, w["w_up"], w["w_down"], x1, w["g_post_ffn"], w["g_pre_ple"])
    y = _ple(hp, w["w_ple_gate"], p.reshape(m, p.shape[-1]), w["w_ple"], x2_, w["g_post_ple"])
    return y.reshape(bsz, t, d), s_new, shift_new, conv_new


def _layer_weights(i, c, g_pre_mix, w_in, mu_shift, w0, w2, a0, a2, g2, k_k, k_a, r_k, lnx_g, lnx_b, w_o_a,
                   dw_w, dw_b, lnc_g, lnc_b, w_o_b, w_out, g_post_mix, g_pre_ffn, w_up, w_down, g_post_ffn,
                   g_pre_ple, w_ple_gate, w_ple, g_post_ple):
    c3 = 3 * c
    n_lora = w2.shape[1] + a2.shape[1] + g2.shape[1]
    n_shift = c3 + n_lora
    lw_pad = -(-n_lora // 128) * 128
    row = lambda z: z[i].reshape(1, -1)
    win = w_in[i]
    return {
        "g_pre_mix": row(g_pre_mix),
        "w_in_t": jnp.swapaxes(win, 0, 1),
        "mu_rkv": mu_shift[i, :c3].reshape(1, -1),
        "mu_lora": jnp.pad(mu_shift[i, c3:], (0, lw_pad - n_lora)).reshape(1, -1),
        "w0": row(w0), "a0": row(a0), "k_k": row(k_k), "k_a": row(k_a),
        "w2": w2[i].astype(bf16), "a2": a2[i].astype(bf16), "g2": g2[i].astype(bf16),
        "r_k": row(r_k), "lnx_g": row(lnx_g), "lnx_b": row(lnx_b),
        "w_o_a": _col_tiles(w_o_a[i].astype(bf16)), "w_o_b": _col_tiles(w_o_b[i].astype(bf16)),
        "w_out": w_out[i].astype(bf16),
        "dw_w": dw_w[i], "dw_b": row(dw_b), "lnc_g": row(lnc_g), "lnc_b": row(lnc_b),
        "g_post_mix": row(g_post_mix), "g_pre_ffn": row(g_pre_ffn),
        "w_up": _col_tiles(w_up[i].astype(bf16)), "w_down": w_down[i].astype(bf16),
        "g_post_ffn": row(g_post_ffn), "g_pre_ple": row(g_pre_ple),
        "w_ple_gate": w_ple_gate[i].astype(bf16), "w_ple": w_ple[i].astype(bf16),
        "g_post_ple": row(g_post_ple),
    }


def kernel(x_prompt, x_sample, p_prompt, p_sample, state_wkv, state_shift, state_conv, g_pre_mix, w_in, mu_shift, w0, w2, a0, a2, g2, k_k, k_a, r_k, lnx_g, lnx_b, w_o_a, dw_w, dw_b, lnc_g, lnc_b, w_o_b, w_out, g_post_mix, g_pre_ffn, w_up, w_down, g_post_ffn, g_pre_ple, w_ple_gate, w_ple, g_post_ple):
    depth = w_in.shape[0]
    c = w_o_a.shape[1]
    heads = c // HEAD_SIZE
    bp, tp, _ = x_prompt.shape
    ts = x_sample.shape[1]
    xp, xs = x_prompt, x_sample
    outs = [[] for _ in range(6)]
    for i in range(depth):
        w = _layer_weights(i, c, g_pre_mix, w_in, mu_shift, w0, w2, a0, a2, g2, k_k, k_a, r_k, lnx_g, lnx_b,
                           w_o_a, dw_w, dw_b, lnc_g, lnc_b, w_o_b, w_out, g_post_mix, g_pre_ffn, w_up, w_down,
                           g_post_ffn, g_pre_ple, w_ple_gate, w_ple, g_post_ple)
        xp, s1, s2, s3 = _layer(xp, p_prompt[i],
                                jnp.zeros((bp, heads, HEAD_SIZE, HEAD_SIZE), f32),
                                jnp.zeros((bp, state_shift.shape[2]), f32),
                                jnp.zeros((bp,) + state_conv.shape[2:], f32), w,
                                chunk=min(64, tp))
        xs, t1, t2, t3 = _layer(xs, p_sample[i], state_wkv[i], state_shift[i], state_conv[i], w,
                                chunk=min(64, ts))
        for lst, val in zip(outs, (s1, s2, s3, t1, t2, t3)):
            lst.append(val)
    return (xp, xs) + tuple(jnp.stack(o) for o in outs)
```

```python
import functools

import jax
import jax.numpy as jnp
from jax import lax
from jax.experimental import pallas as pl
from jax.experimental.pallas import tpu as pltpu

f32 = jnp.float32
bf16 = jnp.bfloat16

HEAD_SIZE = 64
CONV_K = 31
CONV_HALO = 32
RMS_EPS = 1e-6
LN_EPS = 1e-5
GN_EPS = 64e-5
KK_EPS = 1e-12
VMEM_LIMIT = 56 * 1024 * 1024
SUBLANES = 8
LANES = 128
WKV_PROBLEMS = 16

_NN = (((1,), (0,)), ((), ()))
_NT = (((1,), (1,)), ((), ()))
_TN = (((0,), (0,)), ((), ()))


def _params(*sem):
    return pltpu.CompilerParams(dimension_semantics=sem, vmem_limit_bytes=VMEM_LIMIT)


def _tile(n, pref):
    t = min(n, pref)
    while n % t:
        t -= 1
    return t


def _dot(a, b, dims=_NN):
    return lax.dot_general(a.astype(bf16), b.astype(bf16), dims, preferred_element_type=f32)


def _split(a):
    hi = a.astype(bf16)
    lo = (a - hi.astype(f32)).astype(bf16)
    return hi, lo


def _dot3(a, b, dims=_NN):
    a_hi, a_lo = _split(a)
    b_hi, b_lo = _split(b)
    d = functools.partial(lax.dot_general, dimension_numbers=dims, preferred_element_type=f32)
    return d(a_hi, b_hi) + (d(a_hi, b_lo) + d(a_lo, b_hi))


def _dot_exact_rhs(a, b_bf16, dims=_NN):
    a_hi, a_lo = _split(a)
    d = functools.partial(lax.dot_general, dimension_numbers=dims, preferred_element_type=f32)
    return d(a_hi, b_bf16) + d(a_lo, b_bf16)


def _sigmoid(x):
    return 1.0 / (1.0 + jnp.exp(-x))


def _rms(x, g):
    return x * lax.rsqrt(jnp.mean(x * x, axis=-1, keepdims=True) + RMS_EPS) * g


def _rms_cast_kernel(x_ref, g_ref, o_ref):
    o_ref[...] = _rms(x_ref[...], g_ref[...]).astype(o_ref.dtype)


def _rms_cast(x, g):
    m, d = x.shape
    tm = _tile(m, 512)
    return pl.pallas_call(
        _rms_cast_kernel,
        out_shape=jax.ShapeDtypeStruct((m, d), bf16),
        grid=(m // tm,),
        in_specs=[pl.BlockSpec((tm, d), lambda i: (i, 0)), pl.BlockSpec((1, d), lambda i: (0, 0))],
        out_specs=pl.BlockSpec((tm, d), lambda i: (i, 0)),
        compiler_params=_params("parallel"),
        name="rms_cast",
    )(x, g)


def _dot_wt(a, wt_ref):
    return lax.dot_general(a, wt_ref[...].astype(bf16), _NT, preferred_element_type=f32)


def _wt_spec(tn, k, row0):
    assert row0 % SUBLANES == 0 and tn % SUBLANES == 0
    return pl.BlockSpec((pl.Element(tn), pl.Element(k)),
                        lambda i, j: (pl.multiple_of(row0 + j * tn, SUBLANES), 0))


def _mm_kernel(a_ref, wt_ref, o_ref, *, act):
    acc = _dot_wt(a_ref[...], wt_ref)
    if act == "sigmoid":
        acc = _sigmoid(acc)
    o_ref[...] = acc.astype(o_ref.dtype)


def _matmul(a, wt, n, row0=0, act=None, out_dtype=f32, tm_pref=1024, tn_pref=512, name="mm"):
    m, k = a.shape
    tm, tn = _tile(m, tm_pref), _tile(n, tn_pref)
    return pl.pallas_call(
        functools.partial(_mm_kernel, act=act),
        out_shape=jax.ShapeDtypeStruct((m, n), out_dtype),
        grid=(m // tm, n // tn),
        in_specs=[pl.BlockSpec((tm, k), lambda i, j: (i, 0)), _wt_spec(tn, k, row0)],
        out_specs=pl.BlockSpec((tm, tn), lambda i, j: (i, j)),
        compiler_params=_params("parallel", "arbitrary"),
        name=name,
    )(a, wt)


def _glu_kernel(a_ref, w1_ref, w2_ref, o_ref):
    a = a_ref[...]
    o_ref[...] = _dot_wt(a, w1_ref) * _sigmoid(_dot_wt(a, w2_ref))


def _matmul_glu(a, wt, n, row0):
    m, k = a.shape
    tm, tn = _tile(m, 1024), _tile(n, 256)
    return pl.pallas_call(
        _glu_kernel,
        out_shape=jax.ShapeDtypeStruct((m, n), f32),
        grid=(m // tm, n // tn),
        in_specs=[pl.BlockSpec((tm, k), lambda i, j: (i, 0)), _wt_spec(tn, k, row0), _wt_spec(tn, k, row0 + n)],
        out_specs=pl.BlockSpec((tm, tn), lambda i, j: (i, j)),
        compiler_params=_params("parallel", "arbitrary"),
        name="mm_glu",
    )(a, wt, wt)


def _merge_kernel(a1_ref, a2_ref, b1_ref, b2_ref, ga_ref, gb_ref, o_ref):
    ya = jnp.dot(a1_ref[...], b1_ref[...], preferred_element_type=f32)
    yb = jnp.dot(a2_ref[...], b2_ref[...], preferred_element_type=f32)
    o_ref[...] = (ga_ref[...] * ya + gb_ref[...] * yb).astype(o_ref.dtype)


def _matmul_merge(a1, a2, b1, b2, gates):
    m, k = a1.shape
    n = b1.shape[1]
    tm, tn = _tile(m, 512), _tile(n, 512)
    nj = n // tn
    return pl.pallas_call(
        _merge_kernel,
        out_shape=jax.ShapeDtypeStruct((m, n), bf16),
        grid=(m // tm, nj),
        in_specs=[pl.BlockSpec((tm, k), lambda i, j: (i, 0)),
                  pl.BlockSpec((tm, k), lambda i, j: (i, 0)),
                  pl.BlockSpec((k, tn), lambda i, j: (0, j)),
                  pl.BlockSpec((k, tn), lambda i, j: (0, j)),
                  pl.BlockSpec((tm, tn), lambda i, j: (i, j)),
                  pl.BlockSpec((tm, tn), lambda i, j: (i, j + nj))],
        out_specs=pl.BlockSpec((tm, tn), lambda i, j: (i, j)),
        compiler_params=_params("parallel", "arbitrary"),
        name="mm_merge",
    )(a1, a2, b1, b2, gates, gates)


ROW_BLOCK = 256


def _row_blocks(ref):
    sub = _tile(ref.shape[0], ROW_BLOCK)
    return [slice(r, r + sub) for r in range(0, ref.shape[0], sub)]


def _accumulate(o_ref, step, part):
    @pl.when(step == 0)
    def _():
        for rows in _row_blocks(o_ref):
            o_ref[rows, :] = part(rows)

    @pl.when(step > 0)
    def _():
        for rows in _row_blocks(o_ref):
            o_ref[rows, :] += part(rows)


_ONCE = pl.Buffered(1)


def _out_proj_kernel(a_ref, b_ref, x_ref, g1_ref, g2_ref, xo_ref, ho_ref):
    kstep = pl.program_id(1)
    _accumulate(xo_ref, kstep, lambda rows: jnp.dot(a_ref[rows, :], b_ref[...], preferred_element_type=f32))

    @pl.when(kstep == pl.num_programs(1) - 1)
    def _():
        for rows in _row_blocks(xo_ref):
            x1 = x_ref[rows, :] + _rms(xo_ref[rows, :], g1_ref[...])
            xo_ref[rows, :] = x1
            ho_ref[rows, :] = _rms(x1, g2_ref[...]).astype(ho_ref.dtype)


def _out_proj(a, b, x, g_post, g_next):
    m, k = a.shape
    n = b.shape[1]
    tm, tk = _tile(m, 512), _tile(k, 512)
    return pl.pallas_call(
        _out_proj_kernel,
        out_shape=(jax.ShapeDtypeStruct((m, n), f32), jax.ShapeDtypeStruct((m, n), bf16)),
        grid=(m // tm, k // tk),
        in_specs=[pl.BlockSpec((tm, tk), lambda i, kk: (i, kk)),
                  pl.BlockSpec((tk, n), lambda i, kk: (kk, 0)),
                  pl.BlockSpec((tm, n), lambda i, kk: (i, 0), pipeline_mode=_ONCE),
                  pl.BlockSpec((1, n), lambda i, kk: (0, 0)),
                  pl.BlockSpec((1, n), lambda i, kk: (0, 0))],
        out_specs=(pl.BlockSpec((tm, n), lambda i, kk: (i, 0)),
                   pl.BlockSpec((tm, n), lambda i, kk: (i, 0))),
        compiler_params=_params("parallel", "arbitrary"),
        name="out_proj",
    )(a, b, x, g_post, g_next)


def _ffn_kernel(h_ref, wu_ref, wd_ref, x_ref, g1_ref, g2_ref, xo_ref, ho_ref):
    fstep = pl.program_id(1)

    def part(rows):
        up = jnp.dot(h_ref[rows, :], wu_ref[...], preferred_element_type=f32)
        act = jnp.square(jnp.maximum(up, 0.0)).astype(bf16)
        return jnp.dot(act, wd_ref[...], preferred_element_type=f32)

    _accumulate(xo_ref, fstep, part)

    @pl.when(fstep == pl.num_programs(1) - 1)
    def _():
        for rows in _row_blocks(xo_ref):
            x2 = x_ref[rows, :] + _rms(xo_ref[rows, :], g1_ref[...])
            xo_ref[rows, :] = x2
            ho_ref[rows, :] = _rms(x2, g2_ref[...]).astype(ho_ref.dtype)


def _ffn(h, w_up, w_down, x, g_post, g_next):
    m, d = h.shape
    dff = w_up.shape[1]
    tm, tf = _tile(m, 512), _tile(dff, 512)
    return pl.pallas_call(
        _ffn_kernel,
        out_shape=(jax.ShapeDtypeStruct((m, d), f32), jax.ShapeDtypeStruct((m, d), bf16)),
        grid=(m // tm, dff // tf),
        in_specs=[pl.BlockSpec((tm, d), lambda i, j: (i, 0), pipeline_mode=_ONCE),
                  pl.BlockSpec((d, tf), lambda i, j: (0, j)),
                  pl.BlockSpec((tf, d), lambda i, j: (j, 0)),
                  pl.BlockSpec((tm, d), lambda i, j: (i, 0), pipeline_mode=_ONCE),
                  pl.BlockSpec((1, d), lambda i, j: (0, 0)),
                  pl.BlockSpec((1, d), lambda i, j: (0, 0))],
        out_specs=(pl.BlockSpec((tm, d), lambda i, j: (i, 0)),
                   pl.BlockSpec((tm, d), lambda i, j: (i, 0))),
        compiler_params=_params("parallel", "arbitrary"),
        name="ffn",
    )(h, w_up, w_down, x, g_post, g_next)


def _ple_kernel(h_ref, wg_ref, p_ref, wp_ref, x_ref, g_ref, o_ref):
    kstep = pl.program_id(1)
    _accumulate(o_ref, kstep, lambda rows: jnp.dot(h_ref[rows, :], wg_ref[...], preferred_element_type=f32))

    @pl.when(kstep == pl.num_programs(1) - 1)
    def _():
        for rows in _row_blocks(o_ref):
            pe = jnp.dot(p_ref[rows, :].astype(bf16), wp_ref[...], preferred_element_type=f32)
            o_ref[rows, :] = x_ref[rows, :] + _rms(pe * _sigmoid(o_ref[rows, :]), g_ref[...])


def _ple(h, w_gate, p, w_ple, x, g_post):
    m, k = h.shape
    n = w_gate.shape[1]
    q = p.shape[1]
    tm, tk = _tile(m, 512), _tile(k, 512)
    return pl.pallas_call(
        _ple_kernel,
        out_shape=jax.ShapeDtypeStruct((m, n), f32),
        grid=(m // tm, k // tk),
        in_specs=[pl.BlockSpec((tm, tk), lambda i, kk: (i, kk)),
                  pl.BlockSpec((tk, n), lambda i, kk: (kk, 0)),
                  pl.BlockSpec((tm, q), lambda i, kk: (i, 0)),
                  pl.BlockSpec((q, n), lambda i, kk: (0, 0), pipeline_mode=_ONCE),
                  pl.BlockSpec((tm, n), lambda i, kk: (i, 0), pipeline_mode=_ONCE),
                  pl.BlockSpec((1, n), lambda i, kk: (0, 0))],
        out_specs=pl.BlockSpec((tm, n), lambda i, kk: (i, 0)),
        compiler_params=_params("parallel", "arbitrary"),
        name="ple",
    )(h, w_gate, p, w_ple, x, g_post)


def _shift(x_ref, carried_ref, above_ref, mu_ref, seg_len, starts_sequence):
    x = x_ref[...]
    rows, w = x.shape
    prev = pltpu.roll(x, 1, 0)
    first = (lax.broadcasted_iota(jnp.int32, (rows, w), 0) % seg_len) == 0
    nseq = carried_ref.shape[0]
    if nseq == 1:
        above = above_ref[above_ref.shape[0] - 1:, :]
        bnd_rows = jnp.broadcast_to(jnp.where(starts_sequence, carried_ref[0], above), (rows, w))
    else:
        bnd_rows = jnp.broadcast_to(carried_ref[...], (nseq, seg_len, w)).reshape(rows, w)
    prev = jnp.where(first, bnd_rows, prev)
    return x + (prev - x) * mu_ref[...]


def _prep_kernel(pr_ref, pk_ref, pv_ref, cr_ref, ck_ref, cv_ref, ar_ref, ak_ref, av_ref, mur_ref, muk_ref, muv_ref,
                 pl_ref, cl_ref, al8_ref, mul_ref, w0_ref, a0_ref, kkw_ref, kaw_ref,
                 w2_ref, a2_ref, g2_ref, e_ref, tri_ref,
                 r_ref, cum_ref, k_ref, v_ref, kk_ref, a_ref, g_ref,
                 tw_ref, al_ref, sg_ref, *, seg_len, tiles_per_seq, lora):
    dl, al_n, gl_n = lora
    starts = (pl.program_id(0) % tiles_per_seq) == 0

    @pl.when(pl.program_id(1) == 0)
    def _():
        xl = _shift(pl_ref, cl_ref, al8_ref, mul_ref, seg_len, starts)
        tw_ref[...] = jnp.tanh(xl[:, :dl]).astype(bf16)
        al_ref[...] = xl[:, dl:dl + al_n].astype(bf16)
        sg_ref[...] = _sigmoid(xl[:, dl + al_n:dl + al_n + gl_n]).astype(bf16)

    r = _shift(pr_ref, cr_ref, ar_ref, mur_ref, seg_len, starts)
    k = _shift(pk_ref, ck_ref, ak_ref, muk_ref, seg_len, starts)
    v = _shift(pv_ref, cv_ref, av_ref, muv_ref, seg_len, starts)

    dec = w0_ref[...] + jnp.dot(tw_ref[...], w2_ref[...], preferred_element_type=f32)
    z = -dec
    softplus = jnp.maximum(z, 0.0) + jnp.log(1.0 + jnp.exp(-jnp.abs(z)))
    lw = -jnp.exp(-softplus - 0.5)
    a = _sigmoid(a0_ref[...] + jnp.dot(al_ref[...], a2_ref[...], preferred_element_type=f32))
    g = jnp.dot(sg_ref[...], g2_ref[...], preferred_element_type=f32)

    kk = k * kkw_ref[...]
    ssq = _dot_exact_rhs(kk * kk, e_ref[...])
    kk = kk / jnp.maximum(jnp.sqrt(ssq), KK_EPS)
    k = k * (1.0 + (a - 1.0) * kaw_ref[...])

    r_ref[...] = r
    lw_hi, lw_lo = _split(lw)
    cum_ref[...] = (jnp.dot(tri_ref[...], lw_hi, preferred_element_type=f32)
                    + jnp.dot(tri_ref[...], lw_lo, preferred_element_type=f32))
    k_ref[...] = k
    v_ref[...] = v
    kk_ref[...] = kk
    a_ref[...] = a
    g_ref[...] = g


def _head_indicator(width):
    h = jnp.arange(width) // HEAD_SIZE
    return (h[:, None] == h[None, :]).astype(bf16)


def _chunk_tri(rows, chunk):
    t = jnp.arange(rows)
    return ((t[:, None] >= t[None, :]) & (t[:, None] // chunk == t[None, :] // chunk)).astype(bf16)


def _prep(pa_rkv, carried_rkv, pa_lora, carried_lora, mu_rkv, mu_lora, w0, a0, k_k, k_a, w2, a2, g2, t, chunk):
    m, c3 = pa_rkv.shape
    c = c3 // 3
    wl = pa_lora.shape[1]
    lora = (w2.shape[0], a2.shape[0], g2.shape[0])
    tm = _tile(m, 256)
    seg_len = min(t, tm)
    nseq, tiles_per_seq = tm // seg_len, t // seg_len
    tc = _tile(c, 512)
    nj = c // tc
    sub = 8
    above_row = lambda i: jnp.maximum(i * (tm // sub) - 1, 0)
    tok = lambda off: pl.BlockSpec((tm, tc), lambda i, j: (i, j + off))
    car = lambda off: pl.BlockSpec((nseq, 1, tc), lambda i, j: (i // tiles_per_seq, 0, j + off))
    abv = lambda off: pl.BlockSpec((sub, tc), lambda i, j: (above_row(i), j + off))
    vec = lambda off: pl.BlockSpec((1, tc), lambda i, j: (0, j + off))
    low = lambda n: pl.BlockSpec((n, tc), lambda i, j: (0, j))
    out = jax.ShapeDtypeStruct((m, c), f32)
    return pl.pallas_call(
        functools.partial(_prep_kernel, seg_len=seg_len, tiles_per_seq=tiles_per_seq, lora=lora),
        out_shape=(out,) * 7,
        grid=(m // tm, nj),
        in_specs=[tok(0), tok(nj), tok(2 * nj), car(0), car(nj), car(2 * nj), abv(0), abv(nj), abv(2 * nj),
                  vec(0), vec(nj), vec(2 * nj),
                  pl.BlockSpec((tm, wl), lambda i, j: (i, 0)),
                  pl.BlockSpec((nseq, 1, wl), lambda i, j: (i // tiles_per_seq, 0, 0)),
                  pl.BlockSpec((sub, wl), lambda i, j: (above_row(i), 0)),
                  pl.BlockSpec((1, wl), lambda i, j: (0, 0)),
                  vec(0), vec(0), vec(0), vec(0),
                  low(lora[0]), low(lora[1]), low(lora[2]),
                  pl.BlockSpec((tc, tc), lambda i, j: (0, 0)),
                  pl.BlockSpec((tm, tm), lambda i, j: (0, 0))],
        out_specs=(pl.BlockSpec((tm, tc), lambda i, j: (i, j)),) * 7,
        scratch_shapes=[pltpu.VMEM((tm, lora[0]), bf16), pltpu.VMEM((tm, lora[1]), bf16),
                        pltpu.VMEM((tm, lora[2]), bf16)],
        compiler_params=_params("parallel", "arbitrary"),
        name="rwkv_prep",
    )(pa_rkv, pa_rkv, pa_rkv, carried_rkv, carried_rkv, carried_rkv, pa_rkv, pa_rkv, pa_rkv,
      mu_rkv, mu_rkv, mu_rkv, pa_lora, carried_lora, pa_lora, mu_lora,
      w0, a0, k_k, k_a, w2, a2, g2, _head_indicator(tc), _chunk_tri(tm, chunk))


def _bd(x, block):
    n = x.shape[1] // block
    lane_blk = lax.broadcasted_iota(jnp.int32, x.shape, 1) // block
    return jnp.concatenate([jnp.where(lane_blk == h, x, 0.0) for h in range(n)], axis=0)


def _diag_blocks(x, block):
    n = x.shape[1] // block
    rows = x.shape[0] // n
    lane_blk = lax.broadcasted_iota(jnp.int32, (rows, x.shape[1]), 1) // block
    out = x[0:rows]
    for h in range(1, n):
        out = jnp.where(lane_blk == h, x[h * rows:(h + 1) * rows], out)
    return out


def _mm(a, b, dims=_NN):
    return lax.dot_general(a.astype(bf16), b.astype(bf16), dims, preferred_element_type=f32)


def _unit_lower_inverse(a_list, chunk):
    h = chunk // 2
    each = range(len(a_list))
    lane = lax.broadcasted_iota(jnp.int32, (h, LANES), 1)
    row = lax.broadcasted_iota(jnp.int32, (h, LANES), 0)
    left = (lane // h) % 2 == 0
    eye = (row == lane % h).astype(f32)
    diag = [jnp.where(left, a[:h], a[h:]) for a in a_list]
    td = [eye + x for x in diag]
    apow = [_mm(x, _bd(x, h)) for x in diag]
    steps = h.bit_length() - 2
    for step in range(steps):
        rhs = [_bd(x, h).astype(bf16) for x in apow]
        if step + 1 < steps:
            both = [_mm(jnp.concatenate([apow[i], td[i]], axis=0), rhs[i]) for i in each]
            apow = [x[:h] for x in both]
            td = [td[i] + both[i][h:] for i in each]
        else:
            td = [td[i] + _mm(td[i], rhs[i]) for i in each]
    z = [_mm(jnp.where(left, a_list[i][h:], 0.0), _bd(td[i], h)) for i in each]
    t22 = [jnp.where(left, pltpu.roll(x, LANES - h, 1), 0.0) for x in td]
    t21 = [_mm(t22[i], _bd(z[i], h)) for i in each]
    return [jnp.concatenate([jnp.where(left, td[i], 0.0), t21[i] + jnp.where(left, 0.0, td[i])], axis=0)
            for i in each]


def _wkv_local(ins, masks, chunk):
    strict, incl, eye_state = masks
    n = HEAD_SIZE
    gn = ins[0][0].shape[1]
    mw = (gn // n) * chunk
    cat = jnp.concatenate
    each = range(len(ins))
    first_row = lax.broadcasted_iota(jnp.int32, (chunk, gn), 0) == 0

    pre = []
    for r, cl, k, v, kk, a in ins:
        cl_last = cl[chunk - 1:chunk, :]
        at = -kk * jnp.exp(jnp.where(first_row, 0.0, pltpu.roll(cl, 1, 0)))
        rt = r * jnp.exp(cl)
        e_inv = jnp.exp(-cl)
        e_tail = jnp.exp(cl_last - cl)
        b = kk * a
        pre.append((at, rt, b * e_inv, k * e_inv, b * e_tail, k * e_tail, jnp.exp(cl_last)))

    d = [_mm(cat([at, rt], axis=0), cat([_bd(bt, n), _bd(kt, n)], axis=0), _NT)
         for at, rt, bt, kt, _, _, _ in pre]
    a_ab = [jnp.where(strict, x[:chunk, :mw], 0.0) for x in d]
    a_ak = [jnp.where(strict, x[:chunk, mw:], 0.0) for x in d]
    a_rr = [cat([jnp.where(incl, x[chunk:, :mw], 0.0), jnp.where(incl, x[chunk:, mw:], 0.0)], axis=1) for x in d]

    inv = _unit_lower_inverse(a_ab, chunk)

    v_bd = [_bd(x[3], n).astype(bf16) for x in ins]
    rhs1 = [_mm(a_ak[i], v_bd[i]) for i in each]
    x = [_mm(inv[i], cat([_bd(rhs1[i], n), _bd(pre[i][0], n)], axis=1)) for i in each]
    u_loc = [z[:, :gn] for z in x]
    gh = [z[:, gn:] for z in x]
    yq = [_mm(a_rr[i], cat([cat([_bd(u_loc[i], n), _bd(gh[i], n)], axis=1).astype(bf16),
                            cat([v_bd[i], jnp.zeros_like(v_bd[i])], axis=1)], axis=0)) for i in each]
    ms = [_mm(cat([pre[i][4], pre[i][5]], axis=0),
              cat([cat([gh[i], u_loc[i]], axis=1), cat([jnp.zeros_like(ins[i][3]), ins[i][3]], axis=1)], axis=0), _TN)
          for i in each]
    y_loc = [z[:, :gn] for z in yq]
    q = [pre[i][1] + yq[i][:, gn:] for i in each]
    mt = [_diag_blocks(ms[i][:, :gn], n) + eye_state * pre[i][6] for i in each]
    sloc = [_diag_blocks(z[:, gn:], n) for z in ms]
    return y_loc, q, mt, sloc


def _wkv_kernel(r_ref, cum_ref, k_ref, v_ref, kk_ref, a_ref, g_ref, rk_ref, lng_ref, lnb_ref, s0_ref,
                o_ref, st_ref, state_ref, *, chunk):
    n = HEAD_SIZE
    nb, lt, width = r_ref.shape
    gn = (LANES // chunk) * n
    tstep = pl.program_id(2)

    @pl.when(tstep == 0)
    def _():
        state_ref[...] = s0_ref[...]

    mrow = lax.broadcasted_iota(jnp.int32, (chunk, LANES), 0)
    mcol = lax.broadcasted_iota(jnp.int32, (chunk, LANES), 1) % chunk
    srow = lax.broadcasted_iota(jnp.int32, (n, gn), 0)
    scol = lax.broadcasted_iota(jnp.int32, (n, gn), 1) % n
    masks = (mrow > mcol, mrow >= mcol, (srow == scol).astype(f32))
    irow = lax.broadcasted_iota(jnp.int32, (gn, gn), 0) // n
    icol = lax.broadcasted_iota(jnp.int32, (gn, gn), 1) // n
    head_ind = (irow == icol).astype(bf16)

    chains = [(bi, p) for bi in range(nb) for p in range(width // gn)]
    nchunk = lt // chunk
    probs = [(bi, p, ci) for bi, p in chains for ci in range(nchunk)]
    where = {pr: i for i, pr in enumerate(probs)}

    def index(bi, p, ci):
        return bi, slice(ci * chunk, (ci + 1) * chunk), slice(p * gn, (p + 1) * gn)

    ins = [tuple(ref[index(*pr)] for ref in (r_ref, cum_ref, k_ref, v_ref, kk_ref, a_ref)) for pr in probs]
    y_loc, q, mt, sloc = _wkv_local(ins, masks, chunk)

    st = {ch: state_ref[ch] for ch in chains}
    ys, stats = {}, {}
    for step in range(nchunk + 2):
        if step < nchunk:
            for ch in chains:
                i = where[ch + (step,)]
                mt_hi, mt_lo = _split(mt[i])
                r2 = lax.dot_general(jnp.concatenate([q[i].astype(bf16), mt_hi, mt_lo], axis=0),
                                     _bd(st[ch], n).astype(bf16), _NN, preferred_element_type=f32)
                ys[ch, step] = r2[:chunk] + y_loc[i]
                st[ch] = r2[chunk:chunk + n] + r2[chunk + n:] + sloc[i]
        if 0 <= step - 1 < nchunk:
            parts = []
            for ch in chains:
                idx = index(*ch, step - 1)
                y_hi, y_lo = _split(ys[ch, step - 1])
                parts += [y_hi, y_lo, (r_ref[idx] * k_ref[idx] * rk_ref[:, idx[2]]).astype(bf16)]
            sums = lax.dot_general(jnp.concatenate(parts, axis=0), head_ind, _NN, preferred_element_type=f32)
            for ci, ch in enumerate(chains):
                stats[ch, step - 1] = sums[3 * chunk * ci:3 * chunk * (ci + 1)]
        if 0 <= step - 2 < nchunk:
            dlts = []
            for ch in chains:
                sm = stats[ch, step - 2]
                dlts.append(ys.pop((ch, step - 2)) - (sm[:chunk] + sm[chunk:2 * chunk]) * (1.0 / n))
            sq = jnp.concatenate([(x * x).astype(bf16) for x in dlts], axis=0)
            var = lax.dot_general(sq, head_ind, _NN, preferred_element_type=f32) * (1.0 / n)
            for ci, ch in enumerate(chains):
                idx = index(*ch, step - 2)
                lanes = idx[2]
                sm = stats.pop((ch, step - 2))
                out = (dlts[ci] * lax.rsqrt(var[chunk * ci:chunk * (ci + 1)] + GN_EPS) * lng_ref[:, lanes]
                       + lnb_ref[:, lanes] + sm[2 * chunk:] * v_ref[idx])
                o_ref[idx] = (out * g_ref[idx]).astype(o_ref.dtype)
    for ch in chains:
        state_ref[ch] = st[ch]

    @pl.when(tstep == pl.num_programs(2) - 1)
    def _():
        st_ref[...] = state_ref[...]


def _wkv(r, lw, k, v, kk, a, g, r_k, lnx_g, lnx_b, s0, chunk, rows_per_step, seq_per_step, lanes_per_step):
    bsz, t, c = r.shape
    n = HEAD_SIZE
    grp = LANES // chunk
    gn = grp * n
    heads = c // n
    nb, lt, width = seq_per_step, rows_per_step, lanes_per_step
    pg = width // gn
    s0k = s0.reshape(bsz, heads // grp, grp, n, n).transpose(0, 1, 4, 2, 3).reshape(bsz, heads // grp, n, gn)
    tok = pl.BlockSpec((nb, lt, width), lambda b, gi, ti: (b, ti, gi))
    vec = pl.BlockSpec((1, width), lambda b, gi, ti: (0, gi))
    st = pl.BlockSpec((nb, pg, n, gn), lambda b, gi, ti: (b, gi, 0, 0))
    out, stk = pl.pallas_call(
        functools.partial(_wkv_kernel, chunk=chunk),
        out_shape=(jax.ShapeDtypeStruct((bsz, t, c), bf16), jax.ShapeDtypeStruct(s0k.shape, f32)),
        grid=(bsz // nb, c // width, t // lt),
        in_specs=[tok] * 7 + [vec] * 3 + [st],
        out_specs=(tok, st),
        scratch_shapes=[pltpu.VMEM((nb, pg, n, gn), f32)],
        compiler_params=_params("parallel", "parallel", "arbitrary"),
        name="wkv",
    )(r, lw, k, v, kk, a, g, r_k, lnx_g, lnx_b, s0k)
    s_new = stk.reshape(bsz, heads // grp, n, grp, n).transpose(0, 1, 3, 4, 2).reshape(bsz, heads, n, n)
    return out, s_new


def _conv_kernel(u_ref, halo_ref, st_ref, w_ref, b_ref, o_ref, ext_ref, *, taps, sub_rows):
    nb, tt, _ = u_ref.shape
    first = pl.program_id(1) == 0

    @pl.when(first)
    def _():
        ext_ref[:, 0:CONV_HALO, :] = st_ref[...]

    @pl.when(jnp.logical_not(first))
    def _():
        ext_ref[:, 0:CONV_HALO, :] = halo_ref[...]

    ext_ref[:, CONV_HALO:CONV_HALO + tt, :] = u_ref[...]
    lead = CONV_HALO - (taps - 1)
    for bi in range(nb):
        for r0 in range(0, tt, sub_rows):
            acc = None
            for b in range(8):
                rows = sub_rows + (8 if b else 0)
                z = None
                for a8 in range(0, CONV_HALO + 8, 8):
                    tap = a8 + b - lead
                    if 0 <= tap < taps:
                        term = w_ref[tap:tap + 1, :] * ext_ref[bi, r0 + a8:r0 + a8 + rows, :]
                        z = term if z is None else z + term
                if z is not None:
                    z = z[b:b + sub_rows] if b else z
                    acc = z if acc is None else acc + z
            o_ref[bi, r0:r0 + sub_rows, :] = acc + b_ref[...]


def _conv(u, hist, dw_w, dw_b, seq_per_tile, rows_per_tile):
    bsz, t, c = u.shape
    nb, tt = seq_per_tile, rows_per_tile
    tc = _tile(c, 128)
    hpt = tt // CONV_HALO
    taps = dw_w.shape[0]
    return pl.pallas_call(
        functools.partial(_conv_kernel, taps=taps, sub_rows=_tile(tt, 128)),
        out_shape=jax.ShapeDtypeStruct((bsz, t, c), f32),
        grid=(bsz // nb, t // tt, c // tc),
        in_specs=[pl.BlockSpec((nb, tt, tc), lambda b, i, j: (b, i, j)),
                  pl.BlockSpec((nb, CONV_HALO, tc), lambda b, i, j: (b, jnp.maximum(i * hpt - 1, 0), j)),
                  pl.BlockSpec((nb, CONV_HALO, tc), lambda b, i, j: (b, 0, j)),
                  pl.BlockSpec((taps, tc), lambda b, i, j: (0, j)),
                  pl.BlockSpec((1, tc), lambda b, i, j: (0, j))],
        out_specs=pl.BlockSpec((nb, tt, tc), lambda b, i, j: (b, i, j)),
        scratch_shapes=[pltpu.VMEM((nb, CONV_HALO + tt, tc), f32)],
        compiler_params=_params("parallel", "arbitrary", "arbitrary"),
        name="dwconv",
    )(u, u, hist, dw_w, dw_b)


def _ln_silu_kernel(x_ref, g_ref, b_ref, o_ref):
    x = x_ref[...]
    mu = jnp.mean(x, axis=-1, keepdims=True)
    d = x - mu
    var = jnp.mean(d * d, axis=-1, keepdims=True)
    y = d * lax.rsqrt(var + LN_EPS) * g_ref[...] + b_ref[...]
    o_ref[...] = (y * _sigmoid(y)).astype(o_ref.dtype)


def _ln_silu(x, g, b):
    m, d = x.shape
    tm = _tile(m, 512)
    return pl.pallas_call(
        _ln_silu_kernel,
        out_shape=jax.ShapeDtypeStruct((m, d), bf16),
        grid=(m // tm,),
        in_specs=[pl.BlockSpec((tm, d), lambda i: (i, 0)), pl.BlockSpec((1, d), lambda i: (0, 0)),
                  pl.BlockSpec((1, d), lambda i: (0, 0))],
        out_specs=pl.BlockSpec((tm, d), lambda i: (i, 0)),
        compiler_params=_params("parallel"),
        name="ln_silu",
    )(x, g, b)


def _layer(x, p, s_wkv, s_shift, s_conv, w, *, chunk):
    bsz, t, d = x.shape
    m = bsz * t
    c = w["lnx_g"].shape[1]
    c3 = 3 * c
    n_shift = s_shift.shape[1]
    n_lora = n_shift - c3
    x2 = x.reshape(m, d)

    h = _rms_cast(x2, w["g_pre_mix"])
    lw_pad = -(-n_lora // 128) * 128
    wt = w["w_in_t"]
    pa_rkv = _matmul(h, wt, c3, name="mm_rkv")
    pa_lora = _matmul(h, wt, lw_pad, row0=c3, tm_pref=512, tn_pref=lw_pad, name="mm_lora")
    u = _matmul_glu(h, wt, c, row0=n_shift)
    gates = _matmul(h, wt, 2 * d, row0=n_shift + 2 * c, act="sigmoid", name="mm_gate")

    carried_lora = jnp.pad(s_shift[:, c3:], ((0, 0), (0, lw_pad - n_lora)))
    prep = _prep(pa_rkv, s_shift[:, None, :c3], pa_lora, carried_lora[:, None, :],
                 w["mu_rkv"], w["mu_lora"], w["w0"], w["a0"], w["k_k"], w["k_a"], w["w2"], w["a2"], w["g2"], t, chunk)
    r, lw, k, v, kk, a, g = (z.reshape(bsz, t, c) for z in prep)
    wkv_rows = _tile(t, 512)
    group_lanes = (LANES // chunk) * HEAD_SIZE
    wkv_lanes = max(2 * LANES, group_lanes)
    problems = (wkv_rows // chunk) * (wkv_lanes // group_lanes)
    ya_in, s_new = _wkv(r, lw, k, v, kk, a, g, w["r_k"], w["lnx_g"], w["lnx_b"], s_wkv, chunk=chunk,
                        rows_per_step=wkv_rows, seq_per_step=_tile(bsz, max(1, WKV_PROBLEMS // problems)),
                        lanes_per_step=wkv_lanes)

    last_rows = lambda z, width: lax.slice(z, (t - 1, 0), (m, width), (t, 1))
    shift_new = jnp.concatenate([last_rows(pa_rkv, c3), last_rows(pa_lora, n_lora)], axis=1)

    u3 = u.reshape(bsz, t, c)
    hist = jnp.pad(s_conv, ((0, 0), (CONV_HALO - s_conv.shape[1], 0), (0, 0)))
    conv_rows = _tile(t, 512)
    cv = _conv(u3, hist, w["dw_w"], w["dw_b"], _tile(bsz, max(1, 512 // conv_rows)), conv_rows)
    cv = _ln_silu(cv.reshape(m, c), w["lnc_g"], w["lnc_b"])
    keep = s_conv.shape[1]
    conv_new = jnp.concatenate([s_conv, u3], axis=1)[:, -keep:] if t < keep else u3[:, t - keep:]

    merged = _matmul_merge(ya_in.reshape(m, c), cv, w["w_o_a"], w["w_o_b"], gates)
    x1, hf = _out_proj(merged, w["w_out"], x2, w["g_post_mix"], w["g_pre_ffn"])
    x2_, hp = _ffn(hf, w["w_up"], w["w_down"], x1, w["g_post_ffn"], w["g_pre_ple"])
    y = _ple(hp, w["w_ple_gate"], p.reshape(m, p.shape[-1]), w["w_ple"], x2_, w["g_post_ple"])
    return y.reshape(bsz, t, d), s_new, shift_new, conv_new


def _layer_weights(i, c, g_pre_mix, w_in, mu_shift, w0, w2, a0, a2, g2, k_k, k_a, r_k, lnx_g, lnx_b, w_o_a,
                   dw_w, dw_b, lnc_g, lnc_b, w_o_b, w_out, g_post_mix, g_pre_ffn, w_up, w_down, g_post_ffn,
                   g_pre_ple, w_ple_gate, w_ple, g_post_ple):
    c3 = 3 * c
    n_lora = w2.shape[1] + a2.shape[1] + g2.shape[1]
    n_shift = c3 + n_lora
    lw_pad = -(-n_lora // 128) * 128
    row = lambda z: z[i].reshape(1, -1)
    win = w_in[i]
    return {
        "g_pre_mix": row(g_pre_mix),
        "w_in_t": jnp.swapaxes(win, 0, 1),
        "mu_rkv": mu_shift[i, :c3].reshape(1, -1),
        "mu_lora": jnp.pad(mu_shift[i, c3:], (0, lw_pad - n_lora)).reshape(1, -1),
        "w0": row(w0), "a0": row(a0), "k_k": row(k_k), "k_a": row(k_a),
        "w2": w2[i].astype(bf16), "a2": a2[i].astype(bf16), "g2": g2[i].astype(bf16),
        "r_k": row(r_k), "lnx_g": row(lnx_g), "lnx_b": row(lnx_b),
        "w_o_a": w_o_a[i].astype(bf16), "w_o_b": w_o_b[i].astype(bf16), "w_out": w_out[i].astype(bf16),
        "dw_w": dw_w[i], "dw_b": row(dw_b), "lnc_g": row(lnc_g), "lnc_b": row(lnc_b),
        "g_post_mix": row(g_post_mix), "g_pre_ffn": row(g_pre_ffn),
        "w_up": w_up[i].astype(bf16), "w_down": w_down[i].astype(bf16),
        "g_post_ffn": row(g_post_ffn), "g_pre_ple": row(g_pre_ple),
        "w_ple_gate": w_ple_gate[i].astype(bf16), "w_ple": w_ple[i].astype(bf16),
        "g_post_ple": row(g_post_ple),
    }


def kernel(x_prompt, x_sample, p_prompt, p_sample, state_wkv, state_shift, state_conv, g_pre_mix, w_in, mu_shift, w0, w2, a0, a2, g2, k_k, k_a, r_k, lnx_g, lnx_b, w_o_a, dw_w, dw_b, lnc_g, lnc_b, w_o_b, w_out, g_post_mix, g_pre_ffn, w_up, w_down, g_post_ffn, g_pre_ple, w_ple_gate, w_ple, g_post_ple):
    depth = w_in.shape[0]
    c = w_o_a.shape[1]
    heads = c // HEAD_SIZE
    bp, tp, _ = x_prompt.shape
    ts = x_sample.shape[1]
    xp, xs = x_prompt, x_sample
    outs = [[] for _ in range(6)]
    for i in range(depth):
        w = _layer_weights(i, c, g_pre_mix, w_in, mu_shift, w0, w2, a0, a2, g2, k_k, k_a, r_k, lnx_g, lnx_b,
                           w_o_a, dw_w, dw_b, lnc_g, lnc_b, w_o_b, w_out, g_post_mix, g_pre_ffn, w_up, w_down,
                           g_post_ffn, g_pre_ple, w_ple_gate, w_ple, g_post_ple)
        xp, s1, s2, s3 = _layer(xp, p_prompt[i],
                                jnp.zeros((bp, heads, HEAD_SIZE, HEAD_SIZE), f32),
                                jnp.zeros((bp, state_shift.shape[2]), f32),
                                jnp.zeros((bp,) + state_conv.shape[2:], f32), w,
                                chunk=min(64, tp))
        xs, t1, t2, t3 = _layer(xs, p_sample[i], state_wkv[i], state_shift[i], state_conv[i], w,
                                chunk=min(64, ts))
        for lst, val in zip(outs, (s1, s2, s3, t1, t2, t3)):
            lst.append(val)
    return (xp, xs) + tuple(jnp.stack(o) for o in outs)
```

```python
import functools

import jax
import jax.numpy as jnp
from jax import lax
from jax.experimental import pallas as pl
from jax.experimental.pallas import tpu as pltpu

f32 = jnp.float32
bf16 = jnp.bfloat16

HEAD_SIZE = 64
CONV_K = 31
CONV_HALO = 32
RMS_EPS = 1e-6
LN_EPS = 1e-5
GN_EPS = 64e-5
KK_EPS = 1e-12
VMEM_LIMIT = 56 * 1024 * 1024
SUBLANES = 8
LANES = 128
WKV_PROBLEMS = 16

_NN = (((1,), (0,)), ((), ()))
_NT = (((1,), (1,)), ((), ()))
_TN = (((0,), (0,)), ((), ()))


def _params(*sem):
    return pltpu.CompilerParams(dimension_semantics=sem, vmem_limit_bytes=VMEM_LIMIT)


def _tile(n, pref):
    t = min(n, pref)
    while n % t:
        t -= 1
    return t


def _dot(a, b, dims=_NN):
    return lax.dot_general(a.astype(bf16), b.astype(bf16), dims, preferred_element_type=f32)


def _split(a):
    hi = a.astype(bf16)
    lo = (a - hi.astype(f32)).astype(bf16)
    return hi, lo


def _dot3(a, b, dims=_NN):
    a_hi, a_lo = _split(a)
    b_hi, b_lo = _split(b)
    d = functools.partial(lax.dot_general, dimension_numbers=dims, preferred_element_type=f32)
    return d(a_hi, b_hi) + (d(a_hi, b_lo) + d(a_lo, b_hi))


def _dot_exact_rhs(a, b_bf16, dims=_NN):
    a_hi, a_lo = _split(a)
    d = functools.partial(lax.dot_general, dimension_numbers=dims, preferred_element_type=f32)
    return d(a_hi, b_bf16) + d(a_lo, b_bf16)


def _sigmoid(x):
    return 1.0 / (1.0 + jnp.exp(-x))


def _rms(x, g):
    return x * lax.rsqrt(jnp.mean(x * x, axis=-1, keepdims=True) + RMS_EPS) * g


def _rms_cast_kernel(x_ref, g_ref, o_ref):
    o_ref[...] = _rms(x_ref[...], g_ref[...]).astype(o_ref.dtype)


def _rms_cast(x, g):
    m, d = x.shape
    tm = _tile(m, 512)
    return pl.pallas_call(
        _rms_cast_kernel,
        out_shape=jax.ShapeDtypeStruct((m, d), bf16),
        grid=(m // tm,),
        in_specs=[pl.BlockSpec((tm, d), lambda i: (i, 0)), pl.BlockSpec((1, d), lambda i: (0, 0))],
        out_specs=pl.BlockSpec((tm, d), lambda i: (i, 0)),
        compiler_params=_params("parallel"),
        name="rms_cast",
    )(x, g)


def _dot_wt(a, wt_ref):
    return lax.dot_general(a, wt_ref[...].astype(bf16), _NT, preferred_element_type=f32)


def _wt_spec(tn, k, row0):
    assert row0 % SUBLANES == 0 and tn % SUBLANES == 0
    return pl.BlockSpec((pl.Element(tn), pl.Element(k)),
                        lambda i, j: (pl.multiple_of(row0 + j * tn, SUBLANES), 0))


def _mm_kernel(a_ref, wt_ref, o_ref, *, act):
    acc = _dot_wt(a_ref[...], wt_ref)
    if act == "sigmoid":
        acc = _sigmoid(acc)
    o_ref[...] = acc.astype(o_ref.dtype)


def _matmul(a, wt, n, row0=0, act=None, out_dtype=f32, tm_pref=1024, tn_pref=512, name="mm"):
    m, k = a.shape
    tm, tn = _tile(m, tm_pref), _tile(n, tn_pref)
    return pl.pallas_call(
        functools.partial(_mm_kernel, act=act),
        out_shape=jax.ShapeDtypeStruct((m, n), out_dtype),
        grid=(m // tm, n // tn),
        in_specs=[pl.BlockSpec((tm, k), lambda i, j: (i, 0)), _wt_spec(tn, k, row0)],
        out_specs=pl.BlockSpec((tm, tn), lambda i, j: (i, j)),
        compiler_params=_params("parallel", "arbitrary"),
        name=name,
    )(a, wt)


def _glu_kernel(a_ref, w1_ref, w2_ref, o_ref):
    a = a_ref[...]
    o_ref[...] = _dot_wt(a, w1_ref) * _sigmoid(_dot_wt(a, w2_ref))


def _matmul_glu(a, wt, n, row0):
    m, k = a.shape
    tm, tn = _tile(m, 1024), _tile(n, 256)
    return pl.pallas_call(
        _glu_kernel,
        out_shape=jax.ShapeDtypeStruct((m, n), f32),
        grid=(m // tm, n // tn),
        in_specs=[pl.BlockSpec((tm, k), lambda i, j: (i, 0)), _wt_spec(tn, k, row0), _wt_spec(tn, k, row0 + n)],
        out_specs=pl.BlockSpec((tm, tn), lambda i, j: (i, j)),
        compiler_params=_params("parallel", "arbitrary"),
        name="mm_glu",
    )(a, wt, wt)


def _merge_kernel(a1_ref, a2_ref, b1_ref, b2_ref, ga_ref, gb_ref, o_ref):
    ya = jnp.dot(a1_ref[...], b1_ref[...], preferred_element_type=f32)
    yb = jnp.dot(a2_ref[...], b2_ref[...], preferred_element_type=f32)
    o_ref[...] = (ga_ref[...] * ya + gb_ref[...] * yb).astype(o_ref.dtype)


def _matmul_merge(a1, a2, b1, b2, gates):
    m, k = a1.shape
    n = b1.shape[1]
    tm, tn = _tile(m, 512), _tile(n, 512)
    nj = n // tn
    return pl.pallas_call(
        _merge_kernel,
        out_shape=jax.ShapeDtypeStruct((m, n), bf16),
        grid=(m // tm, nj),
        in_specs=[pl.BlockSpec((tm, k), lambda i, j: (i, 0)),
                  pl.BlockSpec((tm, k), lambda i, j: (i, 0)),
                  pl.BlockSpec((k, tn), lambda i, j: (0, j)),
                  pl.BlockSpec((k, tn), lambda i, j: (0, j)),
                  pl.BlockSpec((tm, tn), lambda i, j: (i, j)),
                  pl.BlockSpec((tm, tn), lambda i, j: (i, j + nj))],
        out_specs=pl.BlockSpec((tm, tn), lambda i, j: (i, j)),
        compiler_params=_params("parallel", "arbitrary"),
        name="mm_merge",
    )(a1, a2, b1, b2, gates, gates)


ROW_BLOCK = 256


def _row_blocks(ref):
    sub = _tile(ref.shape[0], ROW_BLOCK)
    return [slice(r, r + sub) for r in range(0, ref.shape[0], sub)]


COL_BLOCK = 512


def _accumulate(o_ref, step, lhs, rhs_ref):
    n = o_ref.shape[1]
    cb = _tile(n, COL_BLOCK)

    def update(first):
        for rows in _row_blocks(o_ref):
            a = lhs(rows)
            for c0 in range(0, n, cb):
                part = jnp.dot(a, rhs_ref[:, c0:c0 + cb], preferred_element_type=f32)
                if first:
                    o_ref[rows, c0:c0 + cb] = part
                else:
                    o_ref[rows, c0:c0 + cb] += part

    pl.when(step == 0)(lambda: update(True))
    pl.when(step > 0)(lambda: update(False))


_ONCE = pl.Buffered(1)


def _out_proj_kernel(a_ref, b_ref, x_ref, g1_ref, g2_ref, xo_ref, ho_ref):
    kstep = pl.program_id(1)
    _accumulate(xo_ref, kstep, lambda rows: a_ref[rows, :], b_ref)

    @pl.when(kstep == pl.num_programs(1) - 1)
    def _():
        for rows in _row_blocks(xo_ref):
            x1 = x_ref[rows, :] + _rms(xo_ref[rows, :], g1_ref[...])
            xo_ref[rows, :] = x1
            ho_ref[rows, :] = _rms(x1, g2_ref[...]).astype(ho_ref.dtype)


def _out_proj(a, b, x, g_post, g_next):
    m, k = a.shape
    n = b.shape[1]
    tm, tk = _tile(m, 512), _tile(k, 512)
    return pl.pallas_call(
        _out_proj_kernel,
        out_shape=(jax.ShapeDtypeStruct((m, n), f32), jax.ShapeDtypeStruct((m, n), bf16)),
        grid=(m // tm, k // tk),
        in_specs=[pl.BlockSpec((tm, tk), lambda i, kk: (i, kk)),
                  pl.BlockSpec((tk, n), lambda i, kk: (kk, 0)),
                  pl.BlockSpec((tm, n), lambda i, kk: (i, 0), pipeline_mode=_ONCE),
                  pl.BlockSpec((1, n), lambda i, kk: (0, 0)),
                  pl.BlockSpec((1, n), lambda i, kk: (0, 0))],
        out_specs=(pl.BlockSpec((tm, n), lambda i, kk: (i, 0)),
                   pl.BlockSpec((tm, n), lambda i, kk: (i, 0))),
        compiler_params=_params("parallel", "arbitrary"),
        name="out_proj",
    )(a, b, x, g_post, g_next)


def _ffn_kernel(h_ref, wu_ref, wd_ref, x_ref, g1_ref, g2_ref, xo_ref, ho_ref):
    fstep = pl.program_id(1)

    def act(rows):
        up = jnp.dot(h_ref[rows, :], wu_ref[...], preferred_element_type=f32)
        return jnp.square(jnp.maximum(up, 0.0)).astype(bf16)

    _accumulate(xo_ref, fstep, act, wd_ref)

    @pl.when(fstep == pl.num_programs(1) - 1)
    def _():
        for rows in _row_blocks(xo_ref):
            x2 = x_ref[rows, :] + _rms(xo_ref[rows, :], g1_ref[...])
            xo_ref[rows, :] = x2
            ho_ref[rows, :] = _rms(x2, g2_ref[...]).astype(ho_ref.dtype)


def _ffn(h, w_up, w_down, x, g_post, g_next):
    m, d = h.shape
    dff = w_up.shape[1]
    tm, tf = _tile(m, 512), _tile(dff, 512)
    return pl.pallas_call(
        _ffn_kernel,
        out_shape=(jax.ShapeDtypeStruct((m, d), f32), jax.ShapeDtypeStruct((m, d), bf16)),
        grid=(m // tm, dff // tf),
        in_specs=[pl.BlockSpec((tm, d), lambda i, j: (i, 0), pipeline_mode=_ONCE),
                  pl.BlockSpec((d, tf), lambda i, j: (0, j)),
                  pl.BlockSpec((tf, d), lambda i, j: (j, 0)),
                  pl.BlockSpec((tm, d), lambda i, j: (i, 0), pipeline_mode=_ONCE),
                  pl.BlockSpec((1, d), lambda i, j: (0, 0)),
                  pl.BlockSpec((1, d), lambda i, j: (0, 0))],
        out_specs=(pl.BlockSpec((tm, d), lambda i, j: (i, 0)),
                   pl.BlockSpec((tm, d), lambda i, j: (i, 0))),
        compiler_params=_params("parallel", "arbitrary"),
        name="ffn",
    )(h, w_up, w_down, x, g_post, g_next)


def _ple_kernel(h_ref, wg_ref, p_ref, wp_ref, x_ref, g_ref, o_ref):
    kstep = pl.program_id(1)
    _accumulate(o_ref, kstep, lambda rows: h_ref[rows, :], wg_ref)

    @pl.when(kstep == pl.num_programs(1) - 1)
    def _():
        for rows in _row_blocks(o_ref):
            pe = jnp.dot(p_ref[rows, :].astype(bf16), wp_ref[...], preferred_element_type=f32)
            o_ref[rows, :] = x_ref[rows, :] + _rms(pe * _sigmoid(o_ref[rows, :]), g_ref[...])


def _ple(h, w_gate, p, w_ple, x, g_post):
    m, k = h.shape
    n = w_gate.shape[1]
    q = p.shape[1]
    tm, tk = _tile(m, 512), _tile(k, 512)
    return pl.pallas_call(
        _ple_kernel,
        out_shape=jax.ShapeDtypeStruct((m, n), f32),
        grid=(m // tm, k // tk),
        in_specs=[pl.BlockSpec((tm, tk), lambda i, kk: (i, kk)),
                  pl.BlockSpec((tk, n), lambda i, kk: (kk, 0)),
                  pl.BlockSpec((tm, q), lambda i, kk: (i, 0)),
                  pl.BlockSpec((q, n), lambda i, kk: (0, 0), pipeline_mode=_ONCE),
                  pl.BlockSpec((tm, n), lambda i, kk: (i, 0), pipeline_mode=_ONCE),
                  pl.BlockSpec((1, n), lambda i, kk: (0, 0))],
        out_specs=pl.BlockSpec((tm, n), lambda i, kk: (i, 0)),
        compiler_params=_params("parallel", "arbitrary"),
        name="ple",
    )(h, w_gate, p, w_ple, x, g_post)


def _shift(x_ref, carried_ref, above_ref, mu_ref, seg_len, starts_sequence):
    x = x_ref[...]
    rows, w = x.shape
    prev = pltpu.roll(x, 1, 0)
    first = (lax.broadcasted_iota(jnp.int32, (rows, w), 0) % seg_len) == 0
    nseq = carried_ref.shape[0]
    if nseq == 1:
        above = above_ref[above_ref.shape[0] - 1:, :]
        bnd_rows = jnp.broadcast_to(jnp.where(starts_sequence, carried_ref[0], above), (rows, w))
    else:
        bnd_rows = jnp.broadcast_to(carried_ref[...], (nseq, seg_len, w)).reshape(rows, w)
    prev = jnp.where(first, bnd_rows, prev)
    return x + (prev - x) * mu_ref[...]


def _lora_act_kernel(pl_ref, cl_ref, al8_ref, mul_ref, tw_ref, al_ref, sg_ref, *, seg_len, tiles_per_seq):
    dl, al_n, gl_n = tw_ref.shape[1], al_ref.shape[1], sg_ref.shape[1]
    starts = (pl.program_id(0) % tiles_per_seq) == 0
    xl = _shift(pl_ref, cl_ref, al8_ref, mul_ref, seg_len, starts)
    tw_ref[...] = jnp.tanh(xl[:, :dl]).astype(bf16)
    al_ref[...] = xl[:, dl:dl + al_n].astype(bf16)
    sg_ref[...] = _sigmoid(xl[:, dl + al_n:dl + al_n + gl_n]).astype(bf16)


def _rwkv_inputs(r, k, v, tw, al, sg, w0, a0, k_k, k_a, w2, a2, g2, head_ind, tri):
    dec = w0 + jnp.dot(tw, w2, preferred_element_type=f32)
    z = -dec
    softplus = jnp.maximum(z, 0.0) + jnp.log(1.0 + jnp.exp(-jnp.abs(z)))
    lw = -jnp.exp(-softplus - 0.5)
    a = _sigmoid(a0 + jnp.dot(al, a2, preferred_element_type=f32))
    g = jnp.dot(sg, g2, preferred_element_type=f32)
    kk = k * k_k
    ssq = _dot_exact_rhs(kk * kk, head_ind)
    kk = kk / jnp.maximum(jnp.sqrt(ssq), KK_EPS)
    k = k * (1.0 + (a - 1.0) * k_a)
    lw_hi, lw_lo = _split(lw)
    cum = jnp.dot(tri, lw_hi, preferred_element_type=f32) + jnp.dot(tri, lw_lo, preferred_element_type=f32)
    return r, cum, k, v, kk, a, g


def _head_indicator(width):
    h = jnp.arange(width) // HEAD_SIZE
    return (h[:, None] == h[None, :]).astype(bf16)


def _chunk_tri(rows, chunk):
    t = jnp.arange(rows)
    return ((t[:, None] >= t[None, :]) & (t[:, None] // chunk == t[None, :] // chunk)).astype(bf16)


def _lora_act(pa_lora, carried_lora, mu_lora, widths, t):
    m, wl = pa_lora.shape
    tm = _tile(m, 512)
    seg_len = min(t, tm)
    nseq, tiles_per_seq = tm // seg_len, t // seg_len
    return pl.pallas_call(
        functools.partial(_lora_act_kernel, seg_len=seg_len, tiles_per_seq=tiles_per_seq),
        out_shape=tuple(jax.ShapeDtypeStruct((m, n), bf16) for n in widths),
        grid=(m // tm,),
        in_specs=[pl.BlockSpec((tm, wl), lambda i: (i, 0)),
                  pl.BlockSpec((nseq, 1, wl), lambda i: (i // tiles_per_seq, 0, 0)),
                  pl.BlockSpec((SUBLANES, wl), lambda i: (jnp.maximum(i * (tm // SUBLANES) - 1, 0), 0)),
                  pl.BlockSpec((1, wl), lambda i: (0, 0))],
        out_specs=tuple(pl.BlockSpec((tm, n), lambda i: (i, 0)) for n in widths),
        compiler_params=_params("parallel"),
        name="lora_act",
    )(pa_lora, carried_lora, pa_lora, mu_lora)


def _bd(x, block):
    n = x.shape[1] // block
    lane_blk = lax.broadcasted_iota(jnp.int32, x.shape, 1) // block
    return jnp.concatenate([jnp.where(lane_blk == h, x, 0.0) for h in range(n)], axis=0)


def _diag_blocks(x, block):
    n = x.shape[1] // block
    rows = x.shape[0] // n
    lane_blk = lax.broadcasted_iota(jnp.int32, (rows, x.shape[1]), 1) // block
    out = x[0:rows]
    for h in range(1, n):
        out = jnp.where(lane_blk == h, x[h * rows:(h + 1) * rows], out)
    return out


def _mm(a, b, dims=_NN):
    return lax.dot_general(a.astype(bf16), b.astype(bf16), dims, preferred_element_type=f32)


def _unit_lower_inverse(a_list, chunk):
    h = chunk // 2
    each = range(len(a_list))
    lane = lax.broadcasted_iota(jnp.int32, (h, LANES), 1)
    row = lax.broadcasted_iota(jnp.int32, (h, LANES), 0)
    left = (lane // h) % 2 == 0
    eye = (row == lane % h).astype(f32)
    diag = [jnp.where(left, a[:h], a[h:]) for a in a_list]
    td = [eye + x for x in diag]
    apow = [_mm(x, _bd(x, h)) for x in diag]
    steps = h.bit_length() - 2
    for step in range(steps):
        rhs = [_bd(x, h).astype(bf16) for x in apow]
        if step + 1 < steps:
            both = [_mm(jnp.concatenate([apow[i], td[i]], axis=0), rhs[i]) for i in each]
            apow = [x[:h] for x in both]
            td = [td[i] + both[i][h:] for i in each]
        else:
            td = [td[i] + _mm(td[i], rhs[i]) for i in each]
    z = [_mm(jnp.where(left, a_list[i][h:], 0.0), _bd(td[i], h)) for i in each]
    t22 = [jnp.where(left, pltpu.roll(x, LANES - h, 1), 0.0) for x in td]
    t21 = [_mm(t22[i], _bd(z[i], h)) for i in each]
    return [jnp.concatenate([jnp.where(left, td[i], 0.0), t21[i] + jnp.where(left, 0.0, td[i])], axis=0)
            for i in each]


def _wkv_local(ins, masks, chunk):
    strict, incl, eye_state = masks
    n = HEAD_SIZE
    gn = ins[0][0].shape[1]
    mw = (gn // n) * chunk
    cat = jnp.concatenate
    each = range(len(ins))
    first_row = lax.broadcasted_iota(jnp.int32, (chunk, gn), 0) == 0

    pre = []
    for r, cl, k, v, kk, a in ins:
        cl_last = cl[chunk - 1:chunk, :]
        at = -kk * jnp.exp(jnp.where(first_row, 0.0, pltpu.roll(cl, 1, 0)))
        rt = r * jnp.exp(cl)
        e_inv = jnp.exp(-cl)
        e_tail = jnp.exp(cl_last - cl)
        b = kk * a
        pre.append((at, rt, b * e_inv, k * e_inv, b * e_tail, k * e_tail, jnp.exp(cl_last)))

    d = [_mm(cat([at, rt], axis=0), cat([_bd(bt, n), _bd(kt, n)], axis=0), _NT)
         for at, rt, bt, kt, _, _, _ in pre]
    a_ab = [jnp.where(strict, x[:chunk, :mw], 0.0) for x in d]
    a_ak = [jnp.where(strict, x[:chunk, mw:], 0.0) for x in d]
    a_rr = [cat([jnp.where(incl, x[chunk:, :mw], 0.0), jnp.where(incl, x[chunk:, mw:], 0.0)], axis=1) for x in d]

    inv = _unit_lower_inverse(a_ab, chunk)

    v_bd = [_bd(x[3], n).astype(bf16) for x in ins]
    rhs1 = [_mm(a_ak[i], v_bd[i]) for i in each]
    x = [_mm(inv[i], cat([_bd(rhs1[i], n), _bd(pre[i][0], n)], axis=1)) for i in each]
    u_loc = [z[:, :gn] for z in x]
    gh = [z[:, gn:] for z in x]
    yq = [_mm(a_rr[i], cat([cat([_bd(u_loc[i], n), _bd(gh[i], n)], axis=1).astype(bf16),
                            cat([v_bd[i], jnp.zeros_like(v_bd[i])], axis=1)], axis=0)) for i in each]
    ms = [_mm(cat([pre[i][4], pre[i][5]], axis=0),
              cat([cat([gh[i], u_loc[i]], axis=1), cat([jnp.zeros_like(ins[i][3]), ins[i][3]], axis=1)], axis=0), _TN)
          for i in each]
    y_loc = [z[:, :gn] for z in yq]
    q = [pre[i][1] + yq[i][:, gn:] for i in each]
    mt = [_diag_blocks(ms[i][:, :gn], n) + eye_state * pre[i][6] for i in each]
    sloc = [_diag_blocks(z[:, gn:], n) for z in ms]
    return y_loc, q, mt, sloc


def _wkv_kernel(pr_ref, pk_ref, pv_ref, cr_ref, ck_ref, cv_ref, ar_ref, ak_ref, av_ref, mur_ref, muk_ref, muv_ref,
                tw_ref, al_ref, sg_ref, w0_ref, a0_ref, kkw_ref, kaw_ref, w2_ref, a2_ref, g2_ref, ind_ref, tri_ref,
                rk_ref, lng_ref, lnb_ref, s0_ref, o_ref, st_ref, state_ref, *, chunk):
    n = HEAD_SIZE
    nb, lt, width = pr_ref.shape
    rows = nb * lt
    gn = (LANES // chunk) * n
    tstep = pl.program_id(2)

    @pl.when(tstep == 0)
    def _():
        state_ref[...] = s0_ref[...]

    first = (lax.broadcasted_iota(jnp.int32, (rows, width), 0) % lt) == 0

    def shifted(x_ref, carried_ref, above_ref, mu_ref):
        x = x_ref[...].reshape(rows, width)
        if nb == 1:
            above = above_ref[0, SUBLANES - 1:, :]
            bnd = jnp.broadcast_to(jnp.where(tstep == 0, carried_ref[0], above), (rows, width))
        else:
            bnd = jnp.broadcast_to(carried_ref[...], (nb, lt, width)).reshape(rows, width)
        prev = jnp.where(first, bnd, pltpu.roll(x, 1, 0))
        return x + (prev - x) * mu_ref[...]

    flat = lambda ref: ref[...].reshape(rows, ref.shape[2])
    r, cum, k, v, kk, a, g = _rwkv_inputs(
        shifted(pr_ref, cr_ref, ar_ref, mur_ref), shifted(pk_ref, ck_ref, ak_ref, muk_ref),
        shifted(pv_ref, cv_ref, av_ref, muv_ref), flat(tw_ref), flat(al_ref), flat(sg_ref),
        w0_ref[...], a0_ref[...], kkw_ref[...], kaw_ref[...], w2_ref[...], a2_ref[...], g2_ref[...],
        ind_ref[...], tri_ref[...])
    bonus_w = r * k * rk_ref[...]

    mrow = lax.broadcasted_iota(jnp.int32, (chunk, LANES), 0)
    mcol = lax.broadcasted_iota(jnp.int32, (chunk, LANES), 1) % chunk
    srow = lax.broadcasted_iota(jnp.int32, (n, gn), 0)
    scol = lax.broadcasted_iota(jnp.int32, (n, gn), 1) % n
    masks = (mrow > mcol, mrow >= mcol, (srow == scol).astype(f32))
    head_ind = ind_ref[0:gn, 0:gn]

    chains = [(bi, p) for bi in range(nb) for p in range(width // gn)]
    nchunk = lt // chunk
    probs = [(bi, p, ci) for bi, p in chains for ci in range(nchunk)]
    where = {pr: i for i, pr in enumerate(probs)}

    def index(bi, p, ci):
        return bi, slice(ci * chunk, (ci + 1) * chunk), slice(p * gn, (p + 1) * gn)

    def at(x, bi, p, ci):
        return x[bi * lt + ci * chunk:bi * lt + (ci + 1) * chunk, p * gn:(p + 1) * gn]

    ins = [tuple(at(x, *pr) for x in (r, cum, k, v, kk, a)) for pr in probs]
    y_loc, q, mt, sloc = _wkv_local(ins, masks, chunk)

    st = {ch: state_ref[ch] for ch in chains}
    ys, stats = {}, {}
    for step in range(nchunk + 2):
        if step < nchunk:
            for ch in chains:
                i = where[ch + (step,)]
                mt_hi, mt_lo = _split(mt[i])
                r2 = lax.dot_general(jnp.concatenate([q[i].astype(bf16), mt_hi, mt_lo], axis=0),
                                     _bd(st[ch], n).astype(bf16), _NN, preferred_element_type=f32)
                ys[ch, step] = r2[:chunk] + y_loc[i]
                st[ch] = r2[chunk:chunk + n] + r2[chunk + n:] + sloc[i]
        if 0 <= step - 1 < nchunk:
            parts = []
            for ch in chains:
                y_hi, y_lo = _split(ys[ch, step - 1])
                parts += [y_hi, y_lo, at(bonus_w, *ch, step - 1).astype(bf16)]
            sums = lax.dot_general(jnp.concatenate(parts, axis=0), head_ind, _NN, preferred_element_type=f32)
            for ci, ch in enumerate(chains):
                stats[ch, step - 1] = sums[3 * chunk * ci:3 * chunk * (ci + 1)]
        if 0 <= step - 2 < nchunk:
            dlts = []
            for ch in chains:
                sm = stats[ch, step - 2]
                dlts.append(ys.pop((ch, step - 2)) - (sm[:chunk] + sm[chunk:2 * chunk]) * (1.0 / n))
            sq = jnp.concatenate([(x * x).astype(bf16) for x in dlts], axis=0)
            var = lax.dot_general(sq, head_ind, _NN, preferred_element_type=f32) * (1.0 / n)
            for ci, ch in enumerate(chains):
                idx = index(*ch, step - 2)
                lanes = idx[2]
                sm = stats.pop((ch, step - 2))
                out = (dlts[ci] * lax.rsqrt(var[chunk * ci:chunk * (ci + 1)] + GN_EPS) * lng_ref[:, lanes]
                       + lnb_ref[:, lanes] + sm[2 * chunk:] * at(v, *ch, step - 2))
                o_ref[idx] = (out * at(g, *ch, step - 2)).astype(o_ref.dtype)
    for ch in chains:
        state_ref[ch] = st[ch]

    @pl.when(tstep == pl.num_programs(2) - 1)
    def _():
        st_ref[...] = state_ref[...]


def _wkv(pa_rkv, carried, mu_rkv, lora_acts, w, s0, chunk, rows_per_step, seq_per_step, lanes_per_step):
    bsz, t, c3 = pa_rkv.shape
    c = c3 // 3
    n = HEAD_SIZE
    grp = LANES // chunk
    gn = grp * n
    heads = c // n
    nb, lt, width = seq_per_step, rows_per_step, lanes_per_step
    pg = width // gn
    nj = c // width
    s0k = s0.reshape(bsz, heads // grp, grp, n, n).transpose(0, 1, 4, 2, 3).reshape(bsz, heads // grp, n, gn)
    above = lambda ti: jnp.maximum(ti * (lt // SUBLANES) - 1, 0)
    tok = lambda off: pl.BlockSpec((nb, lt, width), lambda b, gi, ti: (b, ti, gi + off))
    car = lambda off: pl.BlockSpec((nb, 1, width), lambda b, gi, ti: (b, 0, gi + off))
    abv = lambda off: pl.BlockSpec((nb, SUBLANES, width), lambda b, gi, ti: (b, above(ti), gi + off))
    vec = lambda off: pl.BlockSpec((1, width), lambda b, gi, ti: (0, gi + off))
    act = lambda z: pl.BlockSpec((nb, lt, z.shape[2]), lambda b, gi, ti: (b, ti, 0))
    low = lambda z: pl.BlockSpec((z.shape[0], width), lambda b, gi, ti: (0, gi))
    const = lambda z: pl.BlockSpec(z.shape, lambda b, gi, ti: (0, 0))
    out_tok = pl.BlockSpec((nb, lt, width), lambda b, gi, ti: (b, ti, gi))
    st = pl.BlockSpec((nb, pg, n, gn), lambda b, gi, ti: (b, gi, 0, 0))
    ind, tri = _head_indicator(width), _chunk_tri(nb * lt, chunk)
    out, stk = pl.pallas_call(
        functools.partial(_wkv_kernel, chunk=chunk),
        out_shape=(jax.ShapeDtypeStruct((bsz, t, c), bf16), jax.ShapeDtypeStruct(s0k.shape, f32)),
        grid=(bsz // nb, nj, t // lt),
        in_specs=[tok(0), tok(nj), tok(2 * nj), car(0), car(nj), car(2 * nj), abv(0), abv(nj), abv(2 * nj),
                  vec(0), vec(nj), vec(2 * nj)] + [act(z) for z in lora_acts]
                 + [vec(0)] * 4 + [low(w["w2"]), low(w["a2"]), low(w["g2"]), const(ind), const(tri)]
                 + [vec(0)] * 3 + [st],
        out_specs=(out_tok, st),
        scratch_shapes=[pltpu.VMEM((nb, pg, n, gn), f32)],
        compiler_params=_params("parallel", "parallel", "arbitrary"),
        name="wkv",
    )(pa_rkv, pa_rkv, pa_rkv, carried, carried, carried, pa_rkv, pa_rkv, pa_rkv, mu_rkv, mu_rkv, mu_rkv,
      *lora_acts, w["w0"], w["a0"], w["k_k"], w["k_a"], w["w2"], w["a2"], w["g2"], ind, tri,
      w["r_k"], w["lnx_g"], w["lnx_b"], s0k)
    s_new = stk.reshape(bsz, heads // grp, n, grp, n).transpose(0, 1, 3, 4, 2).reshape(bsz, heads, n, n)
    return out, s_new


def _conv_kernel(u_ref, halo_ref, st_ref, w_ref, b_ref, o_ref, ext_ref, *, taps, sub_rows):
    nb, tt, _ = u_ref.shape
    first = pl.program_id(1) == 0

    @pl.when(first)
    def _():
        ext_ref[:, 0:CONV_HALO, :] = st_ref[...]

    @pl.when(jnp.logical_not(first))
    def _():
        ext_ref[:, 0:CONV_HALO, :] = halo_ref[...]

    ext_ref[:, CONV_HALO:CONV_HALO + tt, :] = u_ref[...]
    lead = CONV_HALO - (taps - 1)
    for bi in range(nb):
        for r0 in range(0, tt, sub_rows):
            acc = None
            for b in range(8):
                rows = sub_rows + (8 if b else 0)
                z = None
                for a8 in range(0, CONV_HALO + 8, 8):
                    tap = a8 + b - lead
                    if 0 <= tap < taps:
                        term = w_ref[tap:tap + 1, :] * ext_ref[bi, r0 + a8:r0 + a8 + rows, :]
                        z = term if z is None else z + term
                if z is not None:
                    z = z[b:b + sub_rows] if b else z
                    acc = z if acc is None else acc + z
            o_ref[bi, r0:r0 + sub_rows, :] = acc + b_ref[...]


def _conv(u, hist, dw_w, dw_b, seq_per_tile, rows_per_tile):
    bsz, t, c = u.shape
    nb, tt = seq_per_tile, rows_per_tile
    tc = _tile(c, 128)
    hpt = tt // CONV_HALO
    taps = dw_w.shape[0]
    return pl.pallas_call(
        functools.partial(_conv_kernel, taps=taps, sub_rows=_tile(tt, 128)),
        out_shape=jax.ShapeDtypeStruct((bsz, t, c), f32),
        grid=(bsz // nb, t // tt, c // tc),
        in_specs=[pl.BlockSpec((nb, tt, tc), lambda b, i, j: (b, i, j)),
                  pl.BlockSpec((nb, CONV_HALO, tc), lambda b, i, j: (b, jnp.maximum(i * hpt - 1, 0), j)),
                  pl.BlockSpec((nb, CONV_HALO, tc), lambda b, i, j: (b, 0, j)),
                  pl.BlockSpec((taps, tc), lambda b, i, j: (0, j)),
                  pl.BlockSpec((1, tc), lambda b, i, j: (0, j))],
        out_specs=pl.BlockSpec((nb, tt, tc), lambda b, i, j: (b, i, j)),
        scratch_shapes=[pltpu.VMEM((nb, CONV_HALO + tt, tc), f32)],
        compiler_params=_params("parallel", "arbitrary", "arbitrary"),
        name="dwconv",
    )(u, u, hist, dw_w, dw_b)


def _ln_silu_kernel(x_ref, g_ref, b_ref, o_ref):
    x = x_ref[...]
    mu = jnp.mean(x, axis=-1, keepdims=True)
    d = x - mu
    var = jnp.mean(d * d, axis=-1, keepdims=True)
    y = d * lax.rsqrt(var + LN_EPS) * g_ref[...] + b_ref[...]
    o_ref[...] = (y * _sigmoid(y)).astype(o_ref.dtype)


def _ln_silu(x, g, b):
    m, d = x.shape
    tm = _tile(m, 512)
    return pl.pallas_call(
        _ln_silu_kernel,
        out_shape=jax.ShapeDtypeStruct((m, d), bf16),
        grid=(m // tm,),
        in_specs=[pl.BlockSpec((tm, d), lambda i: (i, 0)), pl.BlockSpec((1, d), lambda i: (0, 0)),
                  pl.BlockSpec((1, d), lambda i: (0, 0))],
        out_specs=pl.BlockSpec((tm, d), lambda i: (i, 0)),
        compiler_params=_params("parallel"),
        name="ln_silu",
    )(x, g, b)


def _layer(x, p, s_wkv, s_shift, s_conv, w, *, chunk):
    bsz, t, d = x.shape
    m = bsz * t
    c = w["lnx_g"].shape[1]
    c3 = 3 * c
    n_shift = s_shift.shape[1]
    n_lora = n_shift - c3
    x2 = x.reshape(m, d)

    h = _rms_cast(x2, w["g_pre_mix"])
    lw_pad = -(-n_lora // 128) * 128
    wt = w["w_in_t"]
    pa_rkv = _matmul(h, wt, c3, name="mm_rkv")
    pa_lora = _matmul(h, wt, lw_pad, row0=c3, tm_pref=512, tn_pref=lw_pad, name="mm_lora")
    u = _matmul_glu(h, wt, c, row0=n_shift)
    gates = _matmul(h, wt, 2 * d, row0=n_shift + 2 * c, act="sigmoid", name="mm_gate")

    carried_lora = jnp.pad(s_shift[:, c3:], ((0, 0), (0, lw_pad - n_lora)))
    lora_widths = (w["w2"].shape[0], w["a2"].shape[0], w["g2"].shape[0])
    lora_acts = _lora_act(pa_lora, carried_lora[:, None, :], w["mu_lora"], lora_widths, t)
    wkv_rows = _tile(t, 512)
    group_lanes = (LANES // chunk) * HEAD_SIZE
    wkv_lanes = max(2 * LANES, group_lanes)
    problems = (wkv_rows // chunk) * (wkv_lanes // group_lanes)
    ya_in, s_new = _wkv(pa_rkv.reshape(bsz, t, c3), s_shift[:, None, :c3], w["mu_rkv"],
                        [z.reshape(bsz, t, -1) for z in lora_acts], w, s_wkv, chunk=chunk,
                        rows_per_step=wkv_rows, seq_per_step=_tile(bsz, max(1, WKV_PROBLEMS // problems)),
                        lanes_per_step=wkv_lanes)

    last_rows = lambda z, width: lax.slice(z, (t - 1, 0), (m, width), (t, 1))
    shift_new = jnp.concatenate([last_rows(pa_rkv, c3), last_rows(pa_lora, n_lora)], axis=1)

    u3 = u.reshape(bsz, t, c)
    hist = jnp.pad(s_conv, ((0, 0), (CONV_HALO - s_conv.shape[1], 0), (0, 0)))
    conv_rows = _tile(t, 512)
    cv = _conv(u3, hist, w["dw_w"], w["dw_b"], _tile(bsz, max(1, 512 // conv_rows)), conv_rows)
    cv = _ln_silu(cv.reshape(m, c), w["lnc_g"], w["lnc_b"])
    keep = s_conv.shape[1]
    conv_new = jnp.concatenate([s_conv, u3], axis=1)[:, -keep:] if t < keep else u3[:, t - keep:]

    merged = _matmul_merge(ya_in.reshape(m, c), cv, w["w_o_a"], w["w_o_b"], gates)
    x1, hf = _out_proj(merged, w["w_out"], x2, w["g_post_mix"], w["g_pre_ffn"])
    x2_, hp = _ffn(hf, w["w_up"], w["w_down"], x1, w["g_post_ffn"], w["g_pre_ple"])
    y = _ple(hp, w["w_ple_gate"], p.reshape(m, p.shape[-1]), w["w_ple"], x2_, w["g_post_ple"])
    return y.reshape(bsz, t, d), s_new, shift_new, conv_new


def _layer_weights(i, c, g_pre_mix, w_in, mu_shift, w0, w2, a0, a2, g2, k_k, k_a, r_k, lnx_g, lnx_b, w_o_a,
                   dw_w, dw_b, lnc_g, lnc_b, w_o_b, w_out, g_post_mix, g_pre_ffn, w_up, w_down, g_post_ffn,
                   g_pre_ple, w_ple_gate, w_ple, g_post_ple):
    c3 = 3 * c
    n_lora = w2.shape[1] + a2.shape[1] + g2.shape[1]
    n_shift = c3 + n_lora
    lw_pad = -(-n_lora // 128) * 128
    row = lambda z: z[i].reshape(1, -1)
    win = w_in[i]
    return {
        "g_pre_mix": row(g_pre_mix),
        "w_in_t": jnp.swapaxes(win, 0, 1),
        "mu_rkv": mu_shift[i, :c3].reshape(1, -1),
        "mu_lora": jnp.pad(mu_shift[i, c3:], (0, lw_pad - n_lora)).reshape(1, -1),
        "w0": row(w0), "a0": row(a0), "k_k": row(k_k), "k_a": row(k_a),
        "w2": w2[i].astype(bf16), "a2": a2[i].astype(bf16), "g2": g2[i].astype(bf16),
        "r_k": row(r_k), "lnx_g": row(lnx_g), "lnx_b": row(lnx_b),
        "w_o_a": w_o_a[i].astype(bf16), "w_o_b": w_o_b[i].astype(bf16), "w_out": w_out[i].astype(bf16),
        "dw_w": dw_w[i], "dw_b": row(dw_b), "lnc_g": row(lnc_g), "lnc_b": row(lnc_b),
        "g_post_mix": row(g_post_mix), "g_pre_ffn": row(g_pre_ffn),
        "w_up": w_up[i].astype(bf16), "w_down": w_down[i].astype(bf16),
        "g_post_ffn": row(g_post_ffn), "g_pre_ple": row(g_pre_ple),
        "w_ple_gate": w_ple_gate[i].astype(bf16), "w_ple": w_ple[i].astype(bf16),
        "g_post_ple": row(g_post_ple),
    }


def kernel(x_prompt, x_sample, p_prompt, p_sample, state_wkv, state_shift, state_conv, g_pre_mix, w_in, mu_shift, w0, w2, a0, a2, g2, k_k, k_a, r_k, lnx_g, lnx_b, w_o_a, dw_w, dw_b, lnc_g, lnc_b, w_o_b, w_out, g_post_mix, g_pre_ffn, w_up, w_down, g_post_ffn, g_pre_ple, w_ple_gate, w_ple, g_post_ple):
    depth = w_in.shape[0]
    c = w_o_a.shape[1]
    heads = c // HEAD_SIZE
    bp, tp, _ = x_prompt.shape
    ts = x_sample.shape[1]
    xp, xs = x_prompt, x_sample
    outs = [[] for _ in range(6)]
    for i in range(depth):
        w = _layer_weights(i, c, g_pre_mix, w_in, mu_shift, w0, w2, a0, a2, g2, k_k, k_a, r_k, lnx_g, lnx_b,
                           w_o_a, dw_w, dw_b, lnc_g, lnc_b, w_o_b, w_out, g_post_mix, g_pre_ffn, w_up, w_down,
                           g_post_ffn, g_pre_ple, w_ple_gate, w_ple, g_post_ple)
        xp, s1, s2, s3 = _layer(xp, p_prompt[i],
                                jnp.zeros((bp, heads, HEAD_SIZE, HEAD_SIZE), f32),
                                jnp.zeros((bp, state_shift.shape[2]), f32),
                                jnp.zeros((bp,) + state_conv.shape[2:], f32), w,
                                chunk=min(64, tp))
        xs, t1, t2, t3 = _layer(xs, p_sample[i], state_wkv[i], state_shift[i], state_conv[i], w,
                                chunk=min(64, ts))
        for lst, val in zip(outs, (s1, s2, s3, t1, t2, t3)):
            lst.append(val)
    return (xp, xs) + tuple(jnp.stack(o) for o in outs)
```

```python
import functools

import jax
import jax.numpy as jnp
from jax import lax
from jax.experimental import pallas as pl
from jax.experimental.pallas import tpu as pltpu

f32 = jnp.float32
bf16 = jnp.bfloat16

HEAD_SIZE = 64
CONV_K = 31
CONV_HALO = 32
RMS_EPS = 1e-6
LN_EPS = 1e-5
GN_EPS = 64e-5
KK_EPS = 1e-12
VMEM_LIMIT = 56 * 1024 * 1024
SUBLANES = 8
LANES = 128
WKV_ROWS, WKV_LANES = 256, 1024
WKV_PROBLEMS = 16

_NN = (((1,), (0,)), ((), ()))
_NT = (((1,), (1,)), ((), ()))
_TN = (((0,), (0,)), ((), ()))


def _params(*sem):
    return pltpu.CompilerParams(dimension_semantics=sem, vmem_limit_bytes=VMEM_LIMIT)


def _tile(n, pref):
    t = min(n, pref)
    while n % t:
        t -= 1
    return t


def _dot(a, b, dims=_NN):
    return lax.dot_general(a.astype(bf16), b.astype(bf16), dims, preferred_element_type=f32)


def _split(a):
    hi = a.astype(bf16)
    lo = (a - hi.astype(f32)).astype(bf16)
    return hi, lo


def _dot3(a, b, dims=_NN):
    a_hi, a_lo = _split(a)
    b_hi, b_lo = _split(b)
    d = functools.partial(lax.dot_general, dimension_numbers=dims, preferred_element_type=f32)
    return d(a_hi, b_hi) + (d(a_hi, b_lo) + d(a_lo, b_hi))


def _dot_exact_rhs(a, b_bf16, dims=_NN):
    a_hi, a_lo = _split(a)
    d = functools.partial(lax.dot_general, dimension_numbers=dims, preferred_element_type=f32)
    return d(a_hi, b_bf16) + d(a_lo, b_bf16)


def _sigmoid(x):
    return 1.0 / (1.0 + jnp.exp(-x))


def _rms(x, g):
    return x * lax.rsqrt(jnp.mean(x * x, axis=-1, keepdims=True) + RMS_EPS) * g


def _rms_cast_kernel(x_ref, g_ref, o_ref):
    o_ref[...] = _rms(x_ref[...], g_ref[...]).astype(o_ref.dtype)


def _rms_cast(x, g):
    m, d = x.shape
    tm = _tile(m, 512)
    return pl.pallas_call(
        _rms_cast_kernel,
        out_shape=jax.ShapeDtypeStruct((m, d), bf16),
        grid=(m // tm,),
        in_specs=[pl.BlockSpec((tm, d), lambda i: (i, 0)), pl.BlockSpec((1, d), lambda i: (0, 0))],
        out_specs=pl.BlockSpec((tm, d), lambda i: (i, 0)),
        compiler_params=_params("parallel"),
        name="rms_cast",
    )(x, g)


def _dot_wt(a, wt_ref):
    return lax.dot_general(a, wt_ref[...].astype(bf16), _NT, preferred_element_type=f32)


def _wt_spec(tn, k, row0):
    assert row0 % SUBLANES == 0 and tn % SUBLANES == 0
    return pl.BlockSpec((pl.Element(tn), pl.Element(k)),
                        lambda i, j: (pl.multiple_of(row0 + j * tn, SUBLANES), 0))


def _mm_kernel(a_ref, wt_ref, o_ref, *, act):
    acc = _dot_wt(a_ref[...], wt_ref)
    if act == "sigmoid":
        acc = _sigmoid(acc)
    o_ref[...] = acc.astype(o_ref.dtype)


def _matmul(a, wt, n, row0=0, act=None, out_dtype=f32, tm_pref=1024, tn_pref=512, name="mm"):
    m, k = a.shape
    tm, tn = _tile(m, tm_pref), _tile(n, tn_pref)
    return pl.pallas_call(
        functools.partial(_mm_kernel, act=act),
        out_shape=jax.ShapeDtypeStruct((m, n), out_dtype),
        grid=(m // tm, n // tn),
        in_specs=[pl.BlockSpec((tm, k), lambda i, j: (i, 0)), _wt_spec(tn, k, row0)],
        out_specs=pl.BlockSpec((tm, tn), lambda i, j: (i, j)),
        compiler_params=_params("parallel", "arbitrary"),
        name=name,
    )(a, wt)


def _glu_kernel(a_ref, w1_ref, w2_ref, o_ref):
    a = a_ref[...]
    o_ref[...] = _dot_wt(a, w1_ref) * _sigmoid(_dot_wt(a, w2_ref))


def _matmul_glu(a, wt, n, row0):
    m, k = a.shape
    tm, tn = _tile(m, 1024), _tile(n, 256)
    return pl.pallas_call(
        _glu_kernel,
        out_shape=jax.ShapeDtypeStruct((m, n), f32),
        grid=(m // tm, n // tn),
        in_specs=[pl.BlockSpec((tm, k), lambda i, j: (i, 0)), _wt_spec(tn, k, row0), _wt_spec(tn, k, row0 + n)],
        out_specs=pl.BlockSpec((tm, tn), lambda i, j: (i, j)),
        compiler_params=_params("parallel", "arbitrary"),
        name="mm_glu",
    )(a, wt, wt)


def _merge_kernel(a1_ref, a2_ref, b1_ref, b2_ref, ga_ref, gb_ref, o_ref):
    ya = jnp.dot(a1_ref[...], b1_ref[...], preferred_element_type=f32)
    yb = jnp.dot(a2_ref[...], b2_ref[...], preferred_element_type=f32)
    o_ref[...] = (ga_ref[...] * ya + gb_ref[...] * yb).astype(o_ref.dtype)


def _matmul_merge(a1, a2, b1, b2, gates):
    m, k = a1.shape
    n = b1.shape[1]
    tm, tn = _tile(m, 512), _tile(n, 512)
    nj = n // tn
    return pl.pallas_call(
        _merge_kernel,
        out_shape=jax.ShapeDtypeStruct((m, n), bf16),
        grid=(m // tm, nj),
        in_specs=[pl.BlockSpec((tm, k), lambda i, j: (i, 0)),
                  pl.BlockSpec((tm, k), lambda i, j: (i, 0)),
                  pl.BlockSpec((k, tn), lambda i, j: (0, j)),
                  pl.BlockSpec((k, tn), lambda i, j: (0, j)),
                  pl.BlockSpec((tm, tn), lambda i, j: (i, j)),
                  pl.BlockSpec((tm, tn), lambda i, j: (i, j + nj))],
        out_specs=pl.BlockSpec((tm, tn), lambda i, j: (i, j)),
        compiler_params=_params("parallel", "arbitrary"),
        name="mm_merge",
    )(a1, a2, b1, b2, gates, gates)


ROW_BLOCK = 256


def _row_blocks(ref):
    sub = _tile(ref.shape[0], ROW_BLOCK)
    return [slice(r, r + sub) for r in range(0, ref.shape[0], sub)]


COL_BLOCK = 512


def _accumulate(o_ref, step, lhs, rhs_ref):
    n = o_ref.shape[1]
    cb = _tile(n, COL_BLOCK)

    def update(first):
        for rows in _row_blocks(o_ref):
            a = lhs(rows)
            for c0 in range(0, n, cb):
                part = jnp.dot(a, rhs_ref[:, c0:c0 + cb], preferred_element_type=f32)
                if first:
                    o_ref[rows, c0:c0 + cb] = part
                else:
                    o_ref[rows, c0:c0 + cb] += part

    pl.when(step == 0)(lambda: update(True))
    pl.when(step > 0)(lambda: update(False))


_ONCE = pl.Buffered(1)


def _out_proj_kernel(a_ref, b_ref, x_ref, g1_ref, g2_ref, xo_ref, ho_ref):
    kstep = pl.program_id(1)
    _accumulate(xo_ref, kstep, lambda rows: a_ref[rows, :], b_ref)

    @pl.when(kstep == pl.num_programs(1) - 1)
    def _():
        for rows in _row_blocks(xo_ref):
            x1 = x_ref[rows, :] + _rms(xo_ref[rows, :], g1_ref[...])
            xo_ref[rows, :] = x1
            ho_ref[rows, :] = _rms(x1, g2_ref[...]).astype(ho_ref.dtype)


def _out_proj(a, b, x, g_post, g_next):
    m, k = a.shape
    n = b.shape[1]
    tm, tk = _tile(m, 512), _tile(k, 512)
    return pl.pallas_call(
        _out_proj_kernel,
        out_shape=(jax.ShapeDtypeStruct((m, n), f32), jax.ShapeDtypeStruct((m, n), bf16)),
        grid=(m // tm, k // tk),
        in_specs=[pl.BlockSpec((tm, tk), lambda i, kk: (i, kk)),
                  pl.BlockSpec((tk, n), lambda i, kk: (kk, 0)),
                  pl.BlockSpec((tm, n), lambda i, kk: (i, 0), pipeline_mode=_ONCE),
                  pl.BlockSpec((1, n), lambda i, kk: (0, 0)),
                  pl.BlockSpec((1, n), lambda i, kk: (0, 0))],
        out_specs=(pl.BlockSpec((tm, n), lambda i, kk: (i, 0)),
                   pl.BlockSpec((tm, n), lambda i, kk: (i, 0))),
        compiler_params=_params("parallel", "arbitrary"),
        name="out_proj",
    )(a, b, x, g_post, g_next)


def _ffn_kernel(h_ref, wu_ref, wd_ref, x_ref, g1_ref, g2_ref, xo_ref, ho_ref):
    fstep = pl.program_id(1)

    def act(rows):
        up = jnp.dot(h_ref[rows, :], wu_ref[...], preferred_element_type=f32)
        return jnp.square(jnp.maximum(up, 0.0)).astype(bf16)

    _accumulate(xo_ref, fstep, act, wd_ref)

    @pl.when(fstep == pl.num_programs(1) - 1)
    def _():
        for rows in _row_blocks(xo_ref):
            x2 = x_ref[rows, :] + _rms(xo_ref[rows, :], g1_ref[...])
            xo_ref[rows, :] = x2
            ho_ref[rows, :] = _rms(x2, g2_ref[...]).astype(ho_ref.dtype)


def _ffn(h, w_up, w_down, x, g_post, g_next):
    m, d = h.shape
    dff = w_up.shape[1]
    tm, tf = _tile(m, 512), _tile(dff, 512)
    return pl.pallas_call(
        _ffn_kernel,
        out_shape=(jax.ShapeDtypeStruct((m, d), f32), jax.ShapeDtypeStruct((m, d), bf16)),
        grid=(m // tm, dff // tf),
        in_specs=[pl.BlockSpec((tm, d), lambda i, j: (i, 0), pipeline_mode=_ONCE),
                  pl.BlockSpec((d, tf), lambda i, j: (0, j)),
                  pl.BlockSpec((tf, d), lambda i, j: (j, 0)),
                  pl.BlockSpec((tm, d), lambda i, j: (i, 0), pipeline_mode=_ONCE),
                  pl.BlockSpec((1, d), lambda i, j: (0, 0)),
                  pl.BlockSpec((1, d), lambda i, j: (0, 0))],
        out_specs=(pl.BlockSpec((tm, d), lambda i, j: (i, 0)),
                   pl.BlockSpec((tm, d), lambda i, j: (i, 0))),
        compiler_params=_params("parallel", "arbitrary"),
        name="ffn",
    )(h, w_up, w_down, x, g_post, g_next)


def _ple_kernel(h_ref, wg_ref, p_ref, wp_ref, x_ref, g_ref, o_ref):
    kstep = pl.program_id(1)
    _accumulate(o_ref, kstep, lambda rows: h_ref[rows, :], wg_ref)

    @pl.when(kstep == pl.num_programs(1) - 1)
    def _():
        for rows in _row_blocks(o_ref):
            pe = jnp.dot(p_ref[rows, :].astype(bf16), wp_ref[...], preferred_element_type=f32)
            o_ref[rows, :] = x_ref[rows, :] + _rms(pe * _sigmoid(o_ref[rows, :]), g_ref[...])


def _ple(h, w_gate, p, w_ple, x, g_post):
    m, k = h.shape
    n = w_gate.shape[1]
    q = p.shape[1]
    tm, tk = _tile(m, 512), _tile(k, 512)
    return pl.pallas_call(
        _ple_kernel,
        out_shape=jax.ShapeDtypeStruct((m, n), f32),
        grid=(m // tm, k // tk),
        in_specs=[pl.BlockSpec((tm, tk), lambda i, kk: (i, kk)),
                  pl.BlockSpec((tk, n), lambda i, kk: (kk, 0)),
                  pl.BlockSpec((tm, q), lambda i, kk: (i, 0)),
                  pl.BlockSpec((q, n), lambda i, kk: (0, 0), pipeline_mode=_ONCE),
                  pl.BlockSpec((tm, n), lambda i, kk: (i, 0), pipeline_mode=_ONCE),
                  pl.BlockSpec((1, n), lambda i, kk: (0, 0))],
        out_specs=pl.BlockSpec((tm, n), lambda i, kk: (i, 0)),
        compiler_params=_params("parallel", "arbitrary"),
        name="ple",
    )(h, w_gate, p, w_ple, x, g_post)


def _shift(x_ref, carried_ref, above_ref, mu_ref, seg_len, starts_sequence):
    x = x_ref[...]
    rows, w = x.shape
    prev = pltpu.roll(x, 1, 0)
    first = (lax.broadcasted_iota(jnp.int32, (rows, w), 0) % seg_len) == 0
    nseq = carried_ref.shape[0]
    if nseq == 1:
        above = above_ref[above_ref.shape[0] - 1:, :]
        bnd_rows = jnp.broadcast_to(jnp.where(starts_sequence, carried_ref[0], above), (rows, w))
    else:
        bnd_rows = jnp.broadcast_to(carried_ref[...], (nseq, seg_len, w)).reshape(rows, w)
    prev = jnp.where(first, bnd_rows, prev)
    return x + (prev - x) * mu_ref[...]


def _lora_act_kernel(pl_ref, cl_ref, al8_ref, mul_ref, tw_ref, al_ref, sg_ref, *, seg_len, tiles_per_seq):
    dl, al_n, gl_n = tw_ref.shape[1], al_ref.shape[1], sg_ref.shape[1]
    starts = (pl.program_id(0) % tiles_per_seq) == 0
    xl = _shift(pl_ref, cl_ref, al8_ref, mul_ref, seg_len, starts)
    tw_ref[...] = jnp.tanh(xl[:, :dl]).astype(bf16)
    al_ref[...] = xl[:, dl:dl + al_n].astype(bf16)
    sg_ref[...] = _sigmoid(xl[:, dl + al_n:dl + al_n + gl_n]).astype(bf16)


def _rwkv_inputs(r, k, v, tw, al, sg, w0, a0, k_k, k_a, w2, a2, g2, head_ind, tri):
    dec = w0 + jnp.dot(tw, w2, preferred_element_type=f32)
    z = -dec
    softplus = jnp.maximum(z, 0.0) + jnp.log(1.0 + jnp.exp(-jnp.abs(z)))
    lw = -jnp.exp(-softplus - 0.5)
    a = _sigmoid(a0 + jnp.dot(al, a2, preferred_element_type=f32))
    g = jnp.dot(sg, g2, preferred_element_type=f32)
    kk = k * k_k
    ssq = _dot_exact_rhs(kk * kk, head_ind)
    kk = kk / jnp.maximum(jnp.sqrt(ssq), KK_EPS)
    k = k * (1.0 + (a - 1.0) * k_a)
    lw_hi, lw_lo = _split(lw)
    cum = jnp.dot(tri, lw_hi, preferred_element_type=f32) + jnp.dot(tri, lw_lo, preferred_element_type=f32)
    return r, cum, k, v, kk, a, g


def _head_indicator(width):
    h = jnp.arange(width) // HEAD_SIZE
    return (h[:, None] == h[None, :]).astype(bf16)


def _chunk_tri(rows, chunk):
    t = jnp.arange(rows)
    return ((t[:, None] >= t[None, :]) & (t[:, None] // chunk == t[None, :] // chunk)).astype(bf16)


def _lora_act(pa_lora, carried_lora, mu_lora, widths, t):
    m, wl = pa_lora.shape
    tm = _tile(m, 512)
    seg_len = min(t, tm)
    nseq, tiles_per_seq = tm // seg_len, t // seg_len
    return pl.pallas_call(
        functools.partial(_lora_act_kernel, seg_len=seg_len, tiles_per_seq=tiles_per_seq),
        out_shape=tuple(jax.ShapeDtypeStruct((m, n), bf16) for n in widths),
        grid=(m // tm,),
        in_specs=[pl.BlockSpec((tm, wl), lambda i: (i, 0)),
                  pl.BlockSpec((nseq, 1, wl), lambda i: (i // tiles_per_seq, 0, 0)),
                  pl.BlockSpec((SUBLANES, wl), lambda i: (jnp.maximum(i * (tm // SUBLANES) - 1, 0), 0)),
                  pl.BlockSpec((1, wl), lambda i: (0, 0))],
        out_specs=tuple(pl.BlockSpec((tm, n), lambda i: (i, 0)) for n in widths),
        compiler_params=_params("parallel"),
        name="lora_act",
    )(pa_lora, carried_lora, pa_lora, mu_lora)


def _bd(x, block):
    n = x.shape[1] // block
    lane_blk = lax.broadcasted_iota(jnp.int32, x.shape, 1) // block
    return jnp.concatenate([jnp.where(lane_blk == h, x, 0.0) for h in range(n)], axis=0)


def _diag_blocks(x, block):
    n = x.shape[1] // block
    rows = x.shape[0] // n
    lane_blk = lax.broadcasted_iota(jnp.int32, (rows, x.shape[1]), 1) // block
    out = x[0:rows]
    for h in range(1, n):
        out = jnp.where(lane_blk == h, x[h * rows:(h + 1) * rows], out)
    return out


def _mm(a, b, dims=_NN):
    return lax.dot_general(a.astype(bf16), b.astype(bf16), dims, preferred_element_type=f32)


def _unit_lower_inverse(a_list, chunk):
    h = chunk // 2
    each = range(len(a_list))
    lane = lax.broadcasted_iota(jnp.int32, (h, LANES), 1)
    row = lax.broadcasted_iota(jnp.int32, (h, LANES), 0)
    left = (lane // h) % 2 == 0
    eye = (row == lane % h).astype(f32)
    diag = [jnp.where(left, a[:h], a[h:]) for a in a_list]
    td = [eye + x for x in diag]
    apow = [_mm(x, _bd(x, h)) for x in diag]
    steps = h.bit_length() - 2
    for step in range(steps):
        rhs = [_bd(x, h).astype(bf16) for x in apow]
        if step + 1 < steps:
            both = [_mm(jnp.concatenate([apow[i], td[i]], axis=0), rhs[i]) for i in each]
            apow = [x[:h] for x in both]
            td = [td[i] + both[i][h:] for i in each]
        else:
            td = [td[i] + _mm(td[i], rhs[i]) for i in each]
    z = [_mm(jnp.where(left, a_list[i][h:], 0.0), _bd(td[i], h)) for i in each]
    t22 = [jnp.where(left, pltpu.roll(x, LANES - h, 1), 0.0) for x in td]
    t21 = [_mm(t22[i], _bd(z[i], h)) for i in each]
    return [jnp.concatenate([jnp.where(left, td[i], 0.0), t21[i] + jnp.where(left, 0.0, td[i])], axis=0)
            for i in each]


def _wkv_local(ins, masks, chunk):
    strict, incl, eye_state = masks
    n = HEAD_SIZE
    gn = ins[0][0].shape[1]
    mw = (gn // n) * chunk
    cat = jnp.concatenate
    each = range(len(ins))
    first_row = lax.broadcasted_iota(jnp.int32, (chunk, gn), 0) == 0

    pre = []
    for r, cl, k, v, kk, a in ins:
        cl_last = cl[chunk - 1:chunk, :]
        at = -kk * jnp.exp(jnp.where(first_row, 0.0, pltpu.roll(cl, 1, 0)))
        rt = r * jnp.exp(cl)
        e_inv = jnp.exp(-cl)
        e_tail = jnp.exp(cl_last - cl)
        b = kk * a
        pre.append((at, rt, b * e_inv, k * e_inv, b * e_tail, k * e_tail, jnp.exp(cl_last)))

    d = [_mm(cat([at, rt], axis=0), cat([_bd(bt, n), _bd(kt, n)], axis=0), _NT)
         for at, rt, bt, kt, _, _, _ in pre]
    a_ab = [jnp.where(strict, x[:chunk, :mw], 0.0) for x in d]
    a_ak = [jnp.where(strict, x[:chunk, mw:], 0.0) for x in d]
    a_rr = [cat([jnp.where(incl, x[chunk:, :mw], 0.0), jnp.where(incl, x[chunk:, mw:], 0.0)], axis=1) for x in d]

    inv = _unit_lower_inverse(a_ab, chunk)

    v_bd = [_bd(x[3], n).astype(bf16) for x in ins]
    rhs1 = [_mm(a_ak[i], v_bd[i]) for i in each]
    x = [_mm(inv[i], cat([_bd(rhs1[i], n), _bd(pre[i][0], n)], axis=1)) for i in each]
    u_loc = [z[:, :gn] for z in x]
    gh = [z[:, gn:] for z in x]
    yq = [_mm(a_rr[i], cat([cat([_bd(u_loc[i], n), _bd(gh[i], n)], axis=1).astype(bf16),
                            cat([v_bd[i], jnp.zeros_like(v_bd[i])], axis=1)], axis=0)) for i in each]
    ms = [_mm(cat([pre[i][4], pre[i][5]], axis=0),
              cat([cat([gh[i], u_loc[i]], axis=1), cat([jnp.zeros_like(ins[i][3]), ins[i][3]], axis=1)], axis=0), _TN)
          for i in each]
    y_loc = [z[:, :gn] for z in yq]
    q = [pre[i][1] + yq[i][:, gn:] for i in each]
    mt = [_diag_blocks(ms[i][:, :gn], n) + eye_state * pre[i][6] for i in each]
    sloc = [_diag_blocks(z[:, gn:], n) for z in ms]
    return y_loc, q, mt, sloc


def _wkv_kernel(pr_ref, pk_ref, pv_ref, cr_ref, ck_ref, cv_ref, ar_ref, ak_ref, av_ref, mur_ref, muk_ref, muv_ref,
                tw_ref, al_ref, sg_ref, w0_ref, a0_ref, kkw_ref, kaw_ref, w2_ref, a2_ref, g2_ref, ind_ref, tri_ref,
                rk_ref, lng_ref, lnb_ref, s0_ref, o_ref, st_ref, state_ref, *, chunk):
    n = HEAD_SIZE
    nb, lt, width = pr_ref.shape
    rows = nb * lt
    gn = (LANES // chunk) * n
    tstep = pl.program_id(2)

    @pl.when(tstep == 0)
    def _():
        state_ref[...] = s0_ref[...]

    first = (lax.broadcasted_iota(jnp.int32, (rows, width), 0) % lt) == 0

    def shifted(x_ref, carried_ref, above_ref, mu_ref):
        x = x_ref[...].reshape(rows, width)
        if nb == 1:
            above = above_ref[0, SUBLANES - 1:, :]
            bnd = jnp.broadcast_to(jnp.where(tstep == 0, carried_ref[0], above), (rows, width))
        else:
            bnd = jnp.broadcast_to(carried_ref[...], (nb, lt, width)).reshape(rows, width)
        prev = jnp.where(first, bnd, pltpu.roll(x, 1, 0))
        return x + (prev - x) * mu_ref[...]

    flat = lambda ref: ref[...].reshape(rows, ref.shape[2])
    r, cum, k, v, kk, a, g = _rwkv_inputs(
        shifted(pr_ref, cr_ref, ar_ref, mur_ref), shifted(pk_ref, ck_ref, ak_ref, muk_ref),
        shifted(pv_ref, cv_ref, av_ref, muv_ref), flat(tw_ref), flat(al_ref), flat(sg_ref),
        w0_ref[...], a0_ref[...], kkw_ref[...], kaw_ref[...], w2_ref[...], a2_ref[...], g2_ref[...],
        ind_ref[...], tri_ref[...])
    bonus_w = r * k * rk_ref[...]

    mrow = lax.broadcasted_iota(jnp.int32, (chunk, LANES), 0)
    mcol = lax.broadcasted_iota(jnp.int32, (chunk, LANES), 1) % chunk
    srow = lax.broadcasted_iota(jnp.int32, (n, gn), 0)
    scol = lax.broadcasted_iota(jnp.int32, (n, gn), 1) % n
    masks = (mrow > mcol, mrow >= mcol, (srow == scol).astype(f32))
    head_ind = ind_ref[0:gn, 0:gn]

    chains = [(bi, p) for bi in range(nb) for p in range(width // gn)]
    nchunk = lt // chunk
    probs = [(bi, p, ci) for bi, p in chains for ci in range(nchunk)]
    where = {pr: i for i, pr in enumerate(probs)}

    def index(bi, p, ci):
        return bi, slice(ci * chunk, (ci + 1) * chunk), slice(p * gn, (p + 1) * gn)

    def at(x, bi, p, ci):
        return x[bi * lt + ci * chunk:bi * lt + (ci + 1) * chunk, p * gn:(p + 1) * gn]

    ins = [tuple(at(x, *pr) for x in (r, cum, k, v, kk, a)) for pr in probs]
    y_loc, q, mt, sloc = _wkv_local(ins, masks, chunk)

    st = {ch: state_ref[ch] for ch in chains}
    ys, stats = {}, {}
    for step in range(nchunk + 2):
        if step < nchunk:
            for ch in chains:
                i = where[ch + (step,)]
                mt_hi, mt_lo = _split(mt[i])
                r2 = lax.dot_general(jnp.concatenate([q[i].astype(bf16), mt_hi, mt_lo], axis=0),
                                     _bd(st[ch], n).astype(bf16), _NN, preferred_element_type=f32)
                ys[ch, step] = r2[:chunk] + y_loc[i]
                st[ch] = r2[chunk:chunk + n] + r2[chunk + n:] + sloc[i]
        if 0 <= step - 1 < nchunk:
            parts = []
            for ch in chains:
                y_hi, y_lo = _split(ys[ch, step - 1])
                parts += [y_hi, y_lo, at(bonus_w, *ch, step - 1).astype(bf16)]
            sums = lax.dot_general(jnp.concatenate(parts, axis=0), head_ind, _NN, preferred_element_type=f32)
            for ci, ch in enumerate(chains):
                stats[ch, step - 1] = sums[3 * chunk * ci:3 * chunk * (ci + 1)]
        if 0 <= step - 2 < nchunk:
            dlts = []
            for ch in chains:
                sm = stats[ch, step - 2]
                dlts.append(ys.pop((ch, step - 2)) - (sm[:chunk] + sm[chunk:2 * chunk]) * (1.0 / n))
            sq = jnp.concatenate([(x * x).astype(bf16) for x in dlts], axis=0)
            var = lax.dot_general(sq, head_ind, _NN, preferred_element_type=f32) * (1.0 / n)
            for ci, ch in enumerate(chains):
                idx = index(*ch, step - 2)
                lanes = idx[2]
                sm = stats.pop((ch, step - 2))
                out = (dlts[ci] * lax.rsqrt(var[chunk * ci:chunk * (ci + 1)] + GN_EPS) * lng_ref[:, lanes]
                       + lnb_ref[:, lanes] + sm[2 * chunk:] * at(v, *ch, step - 2))
                o_ref[idx] = (out * at(g, *ch, step - 2)).astype(o_ref.dtype)
    for ch in chains:
        state_ref[ch] = st[ch]

    @pl.when(tstep == pl.num_programs(2) - 1)
    def _():
        st_ref[...] = state_ref[...]


def _wkv(pa_rkv, carried, mu_rkv, lora_acts, w, s0, chunk, rows_per_step, seq_per_step, lanes_per_step):
    bsz, t, c3 = pa_rkv.shape
    c = c3 // 3
    n = HEAD_SIZE
    grp = LANES // chunk
    gn = grp * n
    heads = c // n
    nb, lt, width = seq_per_step, rows_per_step, lanes_per_step
    pg = width // gn
    nj = c // width
    s0k = s0.reshape(bsz, heads // grp, grp, n, n).transpose(0, 1, 4, 2, 3).reshape(bsz, heads // grp, n, gn)
    above = lambda ti: jnp.maximum(ti * (lt // SUBLANES) - 1, 0)
    tok = lambda off: pl.BlockSpec((nb, lt, width), lambda b, gi, ti: (b, ti, gi + off))
    car = lambda off: pl.BlockSpec((nb, 1, width), lambda b, gi, ti: (b, 0, gi + off))
    abv = lambda off: pl.BlockSpec((nb, SUBLANES, width), lambda b, gi, ti: (b, above(ti), gi + off))
    vec = lambda off: pl.BlockSpec((1, width), lambda b, gi, ti: (0, gi + off))
    act = lambda z: pl.BlockSpec((nb, lt, z.shape[2]), lambda b, gi, ti: (b, ti, 0))
    low = lambda z: pl.BlockSpec((z.shape[0], width), lambda b, gi, ti: (0, gi))
    const = lambda z: pl.BlockSpec(z.shape, lambda b, gi, ti: (0, 0))
    out_tok = pl.BlockSpec((nb, lt, width), lambda b, gi, ti: (b, ti, gi))
    st = pl.BlockSpec((nb, pg, n, gn), lambda b, gi, ti: (b, gi, 0, 0))
    ind, tri = _head_indicator(width), _chunk_tri(nb * lt, chunk)
    out, stk = pl.pallas_call(
        functools.partial(_wkv_kernel, chunk=chunk),
        out_shape=(jax.ShapeDtypeStruct((bsz, t, c), bf16), jax.ShapeDtypeStruct(s0k.shape, f32)),
        grid=(bsz // nb, nj, t // lt),
        in_specs=[tok(0), tok(nj), tok(2 * nj), car(0), car(nj), car(2 * nj), abv(0), abv(nj), abv(2 * nj),
                  vec(0), vec(nj), vec(2 * nj)] + [act(z) for z in lora_acts]
                 + [vec(0)] * 4 + [low(w["w2"]), low(w["a2"]), low(w["g2"]), const(ind), const(tri)]
                 + [vec(0)] * 3 + [st],
        out_specs=(out_tok, st),
        scratch_shapes=[pltpu.VMEM((nb, pg, n, gn), f32)],
        compiler_params=_params("parallel", "parallel", "arbitrary"),
        name="wkv",
    )(pa_rkv, pa_rkv, pa_rkv, carried, carried, carried, pa_rkv, pa_rkv, pa_rkv, mu_rkv, mu_rkv, mu_rkv,
      *lora_acts, w["w0"], w["a0"], w["k_k"], w["k_a"], w["w2"], w["a2"], w["g2"], ind, tri,
      w["r_k"], w["lnx_g"], w["lnx_b"], s0k)
    s_new = stk.reshape(bsz, heads // grp, n, grp, n).transpose(0, 1, 3, 4, 2).reshape(bsz, heads, n, n)
    return out, s_new


def _conv_kernel(u_ref, halo_ref, st_ref, w_ref, b_ref, o_ref, ext_ref, *, taps, sub_rows):
    nb, tt, _ = u_ref.shape
    first = pl.program_id(1) == 0

    @pl.when(first)
    def _():
        ext_ref[:, 0:CONV_HALO, :] = st_ref[...]

    @pl.when(jnp.logical_not(first))
    def _():
        ext_ref[:, 0:CONV_HALO, :] = halo_ref[...]

    ext_ref[:, CONV_HALO:CONV_HALO + tt, :] = u_ref[...]
    lead = CONV_HALO - (taps - 1)
    for bi in range(nb):
        for r0 in range(0, tt, sub_rows):
            acc = None
            for b in range(8):
                rows = sub_rows + (8 if b else 0)
                z = None
                for a8 in range(0, CONV_HALO + 8, 8):
                    tap = a8 + b - lead
                    if 0 <= tap < taps:
                        term = w_ref[tap:tap + 1, :] * ext_ref[bi, r0 + a8:r0 + a8 + rows, :]
                        z = term if z is None else z + term
                if z is not None:
                    z = z[b:b + sub_rows] if b else z
                    acc = z if acc is None else acc + z
            o_ref[bi, r0:r0 + sub_rows, :] = acc + b_ref[...]


def _conv(u, hist, dw_w, dw_b, seq_per_tile, rows_per_tile):
    bsz, t, c = u.shape
    nb, tt = seq_per_tile, rows_per_tile
    tc = _tile(c, 128)
    hpt = tt // CONV_HALO
    taps = dw_w.shape[0]
    return pl.pallas_call(
        functools.partial(_conv_kernel, taps=taps, sub_rows=_tile(tt, 128)),
        out_shape=jax.ShapeDtypeStruct((bsz, t, c), f32),
        grid=(bsz // nb, t // tt, c // tc),
        in_specs=[pl.BlockSpec((nb, tt, tc), lambda b, i, j: (b, i, j)),
                  pl.BlockSpec((nb, CONV_HALO, tc), lambda b, i, j: (b, jnp.maximum(i * hpt - 1, 0), j)),
                  pl.BlockSpec((nb, CONV_HALO, tc), lambda b, i, j: (b, 0, j)),
                  pl.BlockSpec((taps, tc), lambda b, i, j: (0, j)),
                  pl.BlockSpec((1, tc), lambda b, i, j: (0, j))],
        out_specs=pl.BlockSpec((nb, tt, tc), lambda b, i, j: (b, i, j)),
        scratch_shapes=[pltpu.VMEM((nb, CONV_HALO + tt, tc), f32)],
        compiler_params=_params("parallel", "arbitrary", "arbitrary"),
        name="dwconv",
    )(u, u, hist, dw_w, dw_b)


def _ln_silu_kernel(x_ref, g_ref, b_ref, o_ref):
    x = x_ref[...]
    mu = jnp.mean(x, axis=-1, keepdims=True)
    d = x - mu
    var = jnp.mean(d * d, axis=-1, keepdims=True)
    y = d * lax.rsqrt(var + LN_EPS) * g_ref[...] + b_ref[...]
    o_ref[...] = (y * _sigmoid(y)).astype(o_ref.dtype)


def _ln_silu(x, g, b):
    m, d = x.shape
    tm = _tile(m, 512)
    return pl.pallas_call(
        _ln_silu_kernel,
        out_shape=jax.ShapeDtypeStruct((m, d), bf16),
        grid=(m // tm,),
        in_specs=[pl.BlockSpec((tm, d), lambda i: (i, 0)), pl.BlockSpec((1, d), lambda i: (0, 0)),
                  pl.BlockSpec((1, d), lambda i: (0, 0))],
        out_specs=pl.BlockSpec((tm, d), lambda i: (i, 0)),
        compiler_params=_params("parallel"),
        name="ln_silu",
    )(x, g, b)


def _layer(x, p, s_wkv, s_shift, s_conv, w, *, chunk):
    bsz, t, d = x.shape
    m = bsz * t
    c = w["lnx_g"].shape[1]
    c3 = 3 * c
    n_shift = s_shift.shape[1]
    n_lora = n_shift - c3
    x2 = x.reshape(m, d)

    h = _rms_cast(x2, w["g_pre_mix"])
    lw_pad = -(-n_lora // 128) * 128
    wt = w["w_in_t"]
    pa_rkv = _matmul(h, wt, c3, name="mm_rkv")
    pa_lora = _matmul(h, wt, lw_pad, row0=c3, tm_pref=512, tn_pref=lw_pad, name="mm_lora")
    u = _matmul_glu(h, wt, c, row0=n_shift)
    gates = _matmul(h, wt, 2 * d, row0=n_shift + 2 * c, act="sigmoid", name="mm_gate")

    carried_lora = jnp.pad(s_shift[:, c3:], ((0, 0), (0, lw_pad - n_lora)))
    lora_widths = (w["w2"].shape[0], w["a2"].shape[0], w["g2"].shape[0])
    lora_acts = _lora_act(pa_lora, carried_lora[:, None, :], w["mu_lora"], lora_widths, t)
    wkv_rows = _tile(t, WKV_ROWS)
    group_lanes = (LANES // chunk) * HEAD_SIZE
    wkv_lanes = _tile(c, max(WKV_LANES if wkv_rows > chunk else WKV_LANES // 2, group_lanes))
    problems = (wkv_rows // chunk) * (wkv_lanes // group_lanes)
    ya_in, s_new = _wkv(pa_rkv.reshape(bsz, t, c3), s_shift[:, None, :c3], w["mu_rkv"],
                        [z.reshape(bsz, t, -1) for z in lora_acts], w, s_wkv, chunk=chunk,
                        rows_per_step=wkv_rows, seq_per_step=_tile(bsz, max(1, WKV_PROBLEMS // problems)),
                        lanes_per_step=wkv_lanes)

    last_rows = lambda z, width: lax.slice(z, (t - 1, 0), (m, width), (t, 1))
    shift_new = jnp.concatenate([last_rows(pa_rkv, c3), last_rows(pa_lora, n_lora)], axis=1)

    u3 = u.reshape(bsz, t, c)
    hist = jnp.pad(s_conv, ((0, 0), (CONV_HALO - s_conv.shape[1], 0), (0, 0)))
    conv_rows = _tile(t, 512)
    cv = _conv(u3, hist, w["dw_w"], w["dw_b"], _tile(bsz, max(1, 512 // conv_rows)), conv_rows)
    cv = _ln_silu(cv.reshape(m, c), w["lnc_g"], w["lnc_b"])
    keep = s_conv.shape[1]
    conv_new = jnp.concatenate([s_conv, u3], axis=1)[:, -keep:] if t < keep else u3[:, t - keep:]

    merged = _matmul_merge(ya_in.reshape(m, c), cv, w["w_o_a"], w["w_o_b"], gates)
    x1, hf = _out_proj(merged, w["w_out"], x2, w["g_post_mix"], w["g_pre_ffn"])
    x2_, hp = _ffn(hf, w["w_up"], w["w_down"], x1, w["g_post_ffn"], w["g_pre_ple"])
    y = _ple(hp, w["w_ple_gate"], p.reshape(m, p.shape[-1]), w["w_ple"], x2_, w["g_post_ple"])
    return y.reshape(bsz, t, d), s_new, shift_new, conv_new


def _layer_weights(i, c, g_pre_mix, w_in, mu_shift, w0, w2, a0, a2, g2, k_k, k_a, r_k, lnx_g, lnx_b, w_o_a,
                   dw_w, dw_b, lnc_g, lnc_b, w_o_b, w_out, g_post_mix, g_pre_ffn, w_up, w_down, g_post_ffn,
                   g_pre_ple, w_ple_gate, w_ple, g_post_ple):
    c3 = 3 * c
    n_lora = w2.shape[1] + a2.shape[1] + g2.shape[1]
    n_shift = c3 + n_lora
    lw_pad = -(-n_lora // 128) * 128
    row = lambda z: z[i].reshape(1, -1)
    win = w_in[i]
    return {
        "g_pre_mix": row(g_pre_mix),
        "w_in_t": jnp.swapaxes(win, 0, 1),
        "mu_rkv": mu_shift[i, :c3].reshape(1, -1),
        "mu_lora": jnp.pad(mu_shift[i, c3:], (0, lw_pad - n_lora)).reshape(1, -1),
        "w0": row(w0), "a0": row(a0), "k_k": row(k_k), "k_a": row(k_a),
        "w2": w2[i].astype(bf16), "a2": a2[i].astype(bf16), "g2": g2[i].astype(bf16),
        "r_k": row(r_k), "lnx_g": row(lnx_g), "lnx_b": row(lnx_b),
        "w_o_a": w_o_a[i].astype(bf16), "w_o_b": w_o_b[i].astype(bf16), "w_out": w_out[i].astype(bf16),
        "dw_w": dw_w[i], "dw_b": row(dw_b), "lnc_g": row(lnc_g), "lnc_b": row(lnc_b),
        "g_post_mix": row(g_post_mix), "g_pre_ffn": row(g_pre_ffn),
        "w_up": w_up[i].astype(bf16), "w_down": w_down[i].astype(bf16),
        "g_post_ffn": row(g_post_ffn), "g_pre_ple": row(g_pre_ple),
        "w_ple_gate": w_ple_gate[i].astype(bf16), "w_ple": w_ple[i].astype(bf16),
        "g_post_ple": row(g_post_ple),
    }


def kernel(x_prompt, x_sample, p_prompt, p_sample, state_wkv, state_shift, state_conv, g_pre_mix, w_in, mu_shift, w0, w2, a0, a2, g2, k_k, k_a, r_k, lnx_g, lnx_b, w_o_a, dw_w, dw_b, lnc_g, lnc_b, w_o_b, w_out, g_post_mix, g_pre_ffn, w_up, w_down, g_post_ffn, g_pre_ple, w_ple_gate, w_ple, g_post_ple):
    depth = w_in.shape[0]
    c = w_o_a.shape[1]
    heads = c // HEAD_SIZE
    bp, tp, _ = x_prompt.shape
    ts = x_sample.shape[1]
    xp, xs = x_prompt, x_sample
    outs = [[] for _ in range(6)]
    for i in range(depth):
        w = _layer_weights(i, c, g_pre_mix, w_in, mu_shift, w0, w2, a0, a2, g2, k_k, k_a, r_k, lnx_g, lnx_b,
                           w_o_a, dw_w, dw_b, lnc_g, lnc_b, w_o_b, w_out, g_post_mix, g_pre_ffn, w_up, w_down,
                           g_post_ffn, g_pre_ple, w_ple_gate, w_ple, g_post_ple)
        xp, s1, s2, s3 = _layer(xp, p_prompt[i],
                                jnp.zeros((bp, heads, HEAD_SIZE, HEAD_SIZE), f32),
                                jnp.zeros((bp, state_shift.shape[2]), f32),
                                jnp.zeros((bp,) + state_conv.shape[2:], f32), w,
                                chunk=min(64, tp))
        xs, t1, t2, t3 = _layer(xs, p_sample[i], state_wkv[i], state_shift[i], state_conv[i], w,
                                chunk=min(64, ts))
        for lst, val in zip(outs, (s1, s2, s3, t1, t2, t3)):
            lst.append(val)
    return (xp, xs) + tuple(jnp.stack(o) for o in outs)
```

```python
import functools

import jax
import jax.numpy as jnp
from jax import lax
from jax.experimental import pallas as pl
from jax.experimental.pallas import tpu as pltpu

f32 = jnp.float32
bf16 = jnp.bfloat16

HEAD_SIZE = 64
CONV_K = 31
CONV_HALO = 32
RMS_EPS = 1e-6
LN_EPS = 1e-5
GN_EPS = 64e-5
KK_EPS = 1e-12
VMEM_LIMIT = 56 * 1024 * 1024
SUBLANES = 8
LANES = 128
WKV_ROWS, WKV_LANES = 256, 1024
WKV_PROBLEMS = 16

_NN = (((1,), (0,)), ((), ()))
_NT = (((1,), (1,)), ((), ()))
_TN = (((0,), (0,)), ((), ()))


def _params(*sem):
    return pltpu.CompilerParams(dimension_semantics=sem, vmem_limit_bytes=VMEM_LIMIT)


def _tile(n, pref):
    t = min(n, pref)
    while n % t:
        t -= 1
    return t


def _dot(a, b, dims=_NN):
    return lax.dot_general(a.astype(bf16), b.astype(bf16), dims, preferred_element_type=f32)


def _split(a):
    hi = a.astype(bf16)
    lo = (a - hi.astype(f32)).astype(bf16)
    return hi, lo


def _dot3(a, b, dims=_NN):
    a_hi, a_lo = _split(a)
    b_hi, b_lo = _split(b)
    d = functools.partial(lax.dot_general, dimension_numbers=dims, preferred_element_type=f32)
    return d(a_hi, b_hi) + (d(a_hi, b_lo) + d(a_lo, b_hi))


def _dot_exact_rhs(a, b_bf16, dims=_NN):
    a_hi, a_lo = _split(a)
    d = functools.partial(lax.dot_general, dimension_numbers=dims, preferred_element_type=f32)
    return d(a_hi, b_bf16) + d(a_lo, b_bf16)


def _sigmoid(x):
    return 1.0 / (1.0 + jnp.exp(-x))


def _rms(x, g):
    return x * lax.rsqrt(jnp.mean(x * x, axis=-1, keepdims=True) + RMS_EPS) * g


def _rms_cast_kernel(x_ref, g_ref, o_ref):
    o_ref[...] = _rms(x_ref[...], g_ref[...]).astype(o_ref.dtype)


def _rms_cast(x, g):
    m, d = x.shape
    tm = _tile(m, 512)
    return pl.pallas_call(
        _rms_cast_kernel,
        out_shape=jax.ShapeDtypeStruct((m, d), bf16),
        grid=(m // tm,),
        in_specs=[pl.BlockSpec((tm, d), lambda i: (i, 0)), pl.BlockSpec((1, d), lambda i: (0, 0))],
        out_specs=pl.BlockSpec((tm, d), lambda i: (i, 0)),
        compiler_params=_params("parallel"),
        name="rms_cast",
    )(x, g)


def _dot_wt(a, wt_ref):
    return lax.dot_general(a, wt_ref[...].astype(bf16), _NT, preferred_element_type=f32)


def _wt_spec(tn, k, row0):
    assert row0 % SUBLANES == 0 and tn % SUBLANES == 0
    return pl.BlockSpec((pl.Element(tn), pl.Element(k)),
                        lambda i, j: (pl.multiple_of(row0 + j * tn, SUBLANES), 0))


def _mm_kernel(a_ref, wt_ref, o_ref, *, act):
    acc = _dot_wt(a_ref[...], wt_ref)
    if act == "sigmoid":
        acc = _sigmoid(acc)
    o_ref[...] = acc.astype(o_ref.dtype)


def _matmul(a, wt, n, row0=0, act=None, out_dtype=f32, tm_pref=1024, tn_pref=512, name="mm"):
    m, k = a.shape
    tm, tn = _tile(m, tm_pref), _tile(n, tn_pref)
    return pl.pallas_call(
        functools.partial(_mm_kernel, act=act),
        out_shape=jax.ShapeDtypeStruct((m, n), out_dtype),
        grid=(m // tm, n // tn),
        in_specs=[pl.BlockSpec((tm, k), lambda i, j: (i, 0)), _wt_spec(tn, k, row0)],
        out_specs=pl.BlockSpec((tm, tn), lambda i, j: (i, j)),
        compiler_params=_params("parallel", "arbitrary"),
        name=name,
    )(a, wt)


def _glu_kernel(a_ref, w1_ref, w2_ref, o_ref):
    a = a_ref[...]
    o_ref[...] = _dot_wt(a, w1_ref) * _sigmoid(_dot_wt(a, w2_ref))


def _matmul_glu(a, wt, n, row0):
    m, k = a.shape
    tm, tn = _tile(m, 1024), _tile(n, 256)
    return pl.pallas_call(
        _glu_kernel,
        out_shape=jax.ShapeDtypeStruct((m, n), f32),
        grid=(m // tm, n // tn),
        in_specs=[pl.BlockSpec((tm, k), lambda i, j: (i, 0)), _wt_spec(tn, k, row0), _wt_spec(tn, k, row0 + n)],
        out_specs=pl.BlockSpec((tm, tn), lambda i, j: (i, j)),
        compiler_params=_params("parallel", "arbitrary"),
        name="mm_glu",
    )(a, wt, wt)


def _merge_kernel(a1_ref, a2_ref, b1_ref, b2_ref, ga_ref, gb_ref, o_ref):
    ya = jnp.dot(a1_ref[...], b1_ref[...], preferred_element_type=f32)
    yb = jnp.dot(a2_ref[...], b2_ref[...], preferred_element_type=f32)
    o_ref[...] = (ga_ref[...] * ya + gb_ref[...] * yb).astype(o_ref.dtype)


def _matmul_merge(a1, a2, b1, b2, gates):
    m, k = a1.shape
    n = b1.shape[1]
    tm, tn = _tile(m, 512), _tile(n, 512)
    nj = n // tn
    return pl.pallas_call(
        _merge_kernel,
        out_shape=jax.ShapeDtypeStruct((m, n), bf16),
        grid=(m // tm, nj),
        in_specs=[pl.BlockSpec((tm, k), lambda i, j: (i, 0)),
                  pl.BlockSpec((tm, k), lambda i, j: (i, 0)),
                  pl.BlockSpec((k, tn), lambda i, j: (0, j)),
                  pl.BlockSpec((k, tn), lambda i, j: (0, j)),
                  pl.BlockSpec((tm, tn), lambda i, j: (i, j)),
                  pl.BlockSpec((tm, tn), lambda i, j: (i, j + nj))],
        out_specs=pl.BlockSpec((tm, tn), lambda i, j: (i, j)),
        compiler_params=_params("parallel", "arbitrary"),
        name="mm_merge",
    )(a1, a2, b1, b2, gates, gates)


ROW_BLOCK = 256


def _row_blocks(ref):
    sub = _tile(ref.shape[0], ROW_BLOCK)
    return [slice(r, r + sub) for r in range(0, ref.shape[0], sub)]


COL_BLOCK = 512


def _accumulate(o_ref, step, lhs, rhs_ref):
    n = o_ref.shape[1]
    cb = _tile(n, COL_BLOCK)

    def update(first):
        for rows in _row_blocks(o_ref):
            a = lhs(rows)
            for c0 in range(0, n, cb):
                part = jnp.dot(a, rhs_ref[:, c0:c0 + cb], preferred_element_type=f32)
                if first:
                    o_ref[rows, c0:c0 + cb] = part
                else:
                    o_ref[rows, c0:c0 + cb] += part

    pl.when(step == 0)(lambda: update(True))
    pl.when(step > 0)(lambda: update(False))


_ONCE = pl.Buffered(1)


def _out_proj_kernel(a_ref, b_ref, x_ref, g1_ref, g2_ref, xo_ref, ho_ref):
    kstep = pl.program_id(1)
    _accumulate(xo_ref, kstep, lambda rows: a_ref[rows, :], b_ref)

    @pl.when(kstep == pl.num_programs(1) - 1)
    def _():
        for rows in _row_blocks(xo_ref):
            x1 = x_ref[rows, :] + _rms(xo_ref[rows, :], g1_ref[...])
            xo_ref[rows, :] = x1
            ho_ref[rows, :] = _rms(x1, g2_ref[...]).astype(ho_ref.dtype)


def _out_proj(a, b, x, g_post, g_next):
    m, k = a.shape
    n = b.shape[1]
    tm, tk = _tile(m, 512), _tile(k, 512)
    return pl.pallas_call(
        _out_proj_kernel,
        out_shape=(jax.ShapeDtypeStruct((m, n), f32), jax.ShapeDtypeStruct((m, n), bf16)),
        grid=(m // tm, k // tk),
        in_specs=[pl.BlockSpec((tm, tk), lambda i, kk: (i, kk)),
                  pl.BlockSpec((tk, n), lambda i, kk: (kk, 0)),
                  pl.BlockSpec((tm, n), lambda i, kk: (i, 0)),
                  pl.BlockSpec((1, n), lambda i, kk: (0, 0)),
                  pl.BlockSpec((1, n), lambda i, kk: (0, 0))],
        out_specs=(pl.BlockSpec((tm, n), lambda i, kk: (i, 0)),
                   pl.BlockSpec((tm, n), lambda i, kk: (i, 0))),
        compiler_params=_params("parallel", "arbitrary"),
        name="out_proj",
    )(a, b, x, g_post, g_next)


def _ffn_kernel(h_ref, wu_ref, wd_ref, x_ref, g1_ref, g2_ref, xo_ref, ho_ref):
    fstep = pl.program_id(1)

    def act(rows):
        up = jnp.dot(h_ref[rows, :], wu_ref[...], preferred_element_type=f32)
        return jnp.square(jnp.maximum(up, 0.0)).astype(bf16)

    _accumulate(xo_ref, fstep, act, wd_ref)

    @pl.when(fstep == pl.num_programs(1) - 1)
    def _():
        for rows in _row_blocks(xo_ref):
            x2 = x_ref[rows, :] + _rms(xo_ref[rows, :], g1_ref[...])
            xo_ref[rows, :] = x2
            ho_ref[rows, :] = _rms(x2, g2_ref[...]).astype(ho_ref.dtype)


def _ffn(h, w_up, w_down, x, g_post, g_next):
    m, d = h.shape
    dff = w_up.shape[1]
    tm, tf = _tile(m, 512), _tile(dff, 512)
    return pl.pallas_call(
        _ffn_kernel,
        out_shape=(jax.ShapeDtypeStruct((m, d), f32), jax.ShapeDtypeStruct((m, d), bf16)),
        grid=(m // tm, dff // tf),
        in_specs=[pl.BlockSpec((tm, d), lambda i, j: (i, 0), pipeline_mode=_ONCE),
                  pl.BlockSpec((d, tf), lambda i, j: (0, j)),
                  pl.BlockSpec((tf, d), lambda i, j: (j, 0)),
                  pl.BlockSpec((tm, d), lambda i, j: (i, 0), pipeline_mode=_ONCE),
                  pl.BlockSpec((1, d), lambda i, j: (0, 0)),
                  pl.BlockSpec((1, d), lambda i, j: (0, 0))],
        out_specs=(pl.BlockSpec((tm, d), lambda i, j: (i, 0)),
                   pl.BlockSpec((tm, d), lambda i, j: (i, 0))),
        compiler_params=_params("parallel", "arbitrary"),
        name="ffn",
    )(h, w_up, w_down, x, g_post, g_next)


def _ple_kernel(h_ref, wg_ref, p_ref, wp_ref, x_ref, g_ref, o_ref):
    kstep = pl.program_id(1)
    _accumulate(o_ref, kstep, lambda rows: h_ref[rows, :], wg_ref)

    @pl.when(kstep == pl.num_programs(1) - 1)
    def _():
        for rows in _row_blocks(o_ref):
            pe = jnp.dot(p_ref[rows, :].astype(bf16), wp_ref[...], preferred_element_type=f32)
            o_ref[rows, :] = x_ref[rows, :] + _rms(pe * _sigmoid(o_ref[rows, :]), g_ref[...])


def _ple(h, w_gate, p, w_ple, x, g_post):
    m, k = h.shape
    n = w_gate.shape[1]
    q = p.shape[1]
    tm, tk = _tile(m, 512), _tile(k, 512)
    return pl.pallas_call(
        _ple_kernel,
        out_shape=jax.ShapeDtypeStruct((m, n), f32),
        grid=(m // tm, k // tk),
        in_specs=[pl.BlockSpec((tm, tk), lambda i, kk: (i, kk)),
                  pl.BlockSpec((tk, n), lambda i, kk: (kk, 0)),
                  pl.BlockSpec((tm, q), lambda i, kk: (i, 0)),
                  pl.BlockSpec((q, n), lambda i, kk: (0, 0), pipeline_mode=_ONCE),
                  pl.BlockSpec((tm, n), lambda i, kk: (i, 0)),
                  pl.BlockSpec((1, n), lambda i, kk: (0, 0))],
        out_specs=pl.BlockSpec((tm, n), lambda i, kk: (i, 0)),
        compiler_params=_params("parallel", "arbitrary"),
        name="ple",
    )(h, w_gate, p, w_ple, x, g_post)


def _shift(x_ref, carried_ref, above_ref, mu_ref, seg_len, starts_sequence):
    x = x_ref[...]
    rows, w = x.shape
    prev = pltpu.roll(x, 1, 0)
    first = (lax.broadcasted_iota(jnp.int32, (rows, w), 0) % seg_len) == 0
    nseq = carried_ref.shape[0]
    if nseq == 1:
        above = above_ref[above_ref.shape[0] - 1:, :]
        bnd_rows = jnp.broadcast_to(jnp.where(starts_sequence, carried_ref[0], above), (rows, w))
    else:
        bnd_rows = jnp.broadcast_to(carried_ref[...], (nseq, seg_len, w)).reshape(rows, w)
    prev = jnp.where(first, bnd_rows, prev)
    return x + (prev - x) * mu_ref[...]


def _lora_act_kernel(pl_ref, cl_ref, al8_ref, mul_ref, tw_ref, al_ref, sg_ref, *, seg_len, tiles_per_seq):
    dl, al_n, gl_n = tw_ref.shape[1], al_ref.shape[1], sg_ref.shape[1]
    starts = (pl.program_id(0) % tiles_per_seq) == 0
    xl = _shift(pl_ref, cl_ref, al8_ref, mul_ref, seg_len, starts)
    tw_ref[...] = jnp.tanh(xl[:, :dl]).astype(bf16)
    al_ref[...] = xl[:, dl:dl + al_n].astype(bf16)
    sg_ref[...] = _sigmoid(xl[:, dl + al_n:dl + al_n + gl_n]).astype(bf16)


def _rwkv_inputs(r, k, v, tw, al, sg, w0, a0, k_k, k_a, w2, a2, g2, head_ind, tri):
    dec = w0 + jnp.dot(tw, w2, preferred_element_type=f32)
    z = -dec
    softplus = jnp.maximum(z, 0.0) + jnp.log(1.0 + jnp.exp(-jnp.abs(z)))
    lw = -jnp.exp(-softplus - 0.5)
    a = _sigmoid(a0 + jnp.dot(al, a2, preferred_element_type=f32))
    g = jnp.dot(sg, g2, preferred_element_type=f32)
    kk = k * k_k
    ssq = _dot_exact_rhs(kk * kk, head_ind)
    kk = kk / jnp.maximum(jnp.sqrt(ssq), KK_EPS)
    k = k * (1.0 + (a - 1.0) * k_a)
    lw_hi, lw_lo = _split(lw)
    cum = jnp.dot(tri, lw_hi, preferred_element_type=f32) + jnp.dot(tri, lw_lo, preferred_element_type=f32)
    return r, cum, k, v, kk, a, g


def _head_indicator(width):
    h = jnp.arange(width) // HEAD_SIZE
    return (h[:, None] == h[None, :]).astype(bf16)


def _chunk_tri(rows, chunk):
    t = jnp.arange(rows)
    return ((t[:, None] >= t[None, :]) & (t[:, None] // chunk == t[None, :] // chunk)).astype(bf16)


def _lora_act(pa_lora, carried_lora, mu_lora, widths, t):
    m, wl = pa_lora.shape
    tm = _tile(m, 512)
    seg_len = min(t, tm)
    nseq, tiles_per_seq = tm // seg_len, t // seg_len
    return pl.pallas_call(
        functools.partial(_lora_act_kernel, seg_len=seg_len, tiles_per_seq=tiles_per_seq),
        out_shape=tuple(jax.ShapeDtypeStruct((m, n), bf16) for n in widths),
        grid=(m // tm,),
        in_specs=[pl.BlockSpec((tm, wl), lambda i: (i, 0)),
                  pl.BlockSpec((nseq, 1, wl), lambda i: (i // tiles_per_seq, 0, 0)),
                  pl.BlockSpec((SUBLANES, wl), lambda i: (jnp.maximum(i * (tm // SUBLANES) - 1, 0), 0)),
                  pl.BlockSpec((1, wl), lambda i: (0, 0))],
        out_specs=tuple(pl.BlockSpec((tm, n), lambda i: (i, 0)) for n in widths),
        compiler_params=_params("parallel"),
        name="lora_act",
    )(pa_lora, carried_lora, pa_lora, mu_lora)


def _bd(x, block):
    n = x.shape[1] // block
    lane_blk = lax.broadcasted_iota(jnp.int32, x.shape, 1) // block
    return jnp.concatenate([jnp.where(lane_blk == h, x, 0.0) for h in range(n)], axis=0)


def _diag_blocks(x, block):
    n = x.shape[1] // block
    rows = x.shape[0] // n
    lane_blk = lax.broadcasted_iota(jnp.int32, (rows, x.shape[1]), 1) // block
    out = x[0:rows]
    for h in range(1, n):
        out = jnp.where(lane_blk == h, x[h * rows:(h + 1) * rows], out)
    return out


def _mm(a, b, dims=_NN):
    return lax.dot_general(a.astype(bf16), b.astype(bf16), dims, preferred_element_type=f32)


def _unit_lower_inverse(a_list, chunk):
    h = chunk // 2
    each = range(len(a_list))
    lane = lax.broadcasted_iota(jnp.int32, (h, LANES), 1)
    row = lax.broadcasted_iota(jnp.int32, (h, LANES), 0)
    left = (lane // h) % 2 == 0
    eye = (row == lane % h).astype(f32)
    diag = [jnp.where(left, a[:h], a[h:]) for a in a_list]
    td = [eye + x for x in diag]
    apow = [_mm(x, _bd(x, h)) for x in diag]
    steps = h.bit_length() - 2
    for step in range(steps):
        rhs = [_bd(x, h).astype(bf16) for x in apow]
        if step + 1 < steps:
            both = [_mm(jnp.concatenate([apow[i], td[i]], axis=0), rhs[i]) for i in each]
            apow = [x[:h] for x in both]
            td = [td[i] + both[i][h:] for i in each]
        else:
            td = [td[i] + _mm(td[i], rhs[i]) for i in each]
    z = [_mm(jnp.where(left, a_list[i][h:], 0.0), _bd(td[i], h)) for i in each]
    t22 = [jnp.where(left, pltpu.roll(x, LANES - h, 1), 0.0) for x in td]
    t21 = [_mm(t22[i], _bd(z[i], h)) for i in each]
    return [jnp.concatenate([jnp.where(left, td[i], 0.0), t21[i] + jnp.where(left, 0.0, td[i])], axis=0)
            for i in each]


def _wkv_local(ins, masks, chunk):
    strict, incl, eye_state = masks
    n = HEAD_SIZE
    gn = ins[0][0].shape[1]
    mw = (gn // n) * chunk
    cat = jnp.concatenate
    each = range(len(ins))
    first_row = lax.broadcasted_iota(jnp.int32, (chunk, gn), 0) == 0

    pre = []
    for r, cl, k, v, kk, a in ins:
        cl_last = cl[chunk - 1:chunk, :]
        at = -kk * jnp.exp(jnp.where(first_row, 0.0, pltpu.roll(cl, 1, 0)))
        rt = r * jnp.exp(cl)
        e_inv = jnp.exp(-cl)
        e_tail = jnp.exp(cl_last - cl)
        b = kk * a
        pre.append((at, rt, b * e_inv, k * e_inv, b * e_tail, k * e_tail, jnp.exp(cl_last)))

    d = [_mm(cat([at, rt], axis=0), cat([_bd(bt, n), _bd(kt, n)], axis=0), _NT)
         for at, rt, bt, kt, _, _, _ in pre]
    a_ab = [jnp.where(strict, x[:chunk, :mw], 0.0) for x in d]
    a_ak = [jnp.where(strict, x[:chunk, mw:], 0.0) for x in d]
    a_rr = [cat([jnp.where(incl, x[chunk:, :mw], 0.0), jnp.where(incl, x[chunk:, mw:], 0.0)], axis=1) for x in d]

    inv = _unit_lower_inverse(a_ab, chunk)

    v_bd = [_bd(x[3], n).astype(bf16) for x in ins]
    rhs1 = [_mm(a_ak[i], v_bd[i]) for i in each]
    x = [_mm(inv[i], cat([_bd(rhs1[i], n), _bd(pre[i][0], n)], axis=1)) for i in each]
    u_loc = [z[:, :gn] for z in x]
    gh = [z[:, gn:] for z in x]
    yq = [_mm(a_rr[i], cat([cat([_bd(u_loc[i], n), _bd(gh[i], n)], axis=1).astype(bf16),
                            cat([v_bd[i], jnp.zeros_like(v_bd[i])], axis=1)], axis=0)) for i in each]
    ms = [_mm(cat([pre[i][4], pre[i][5]], axis=0),
              cat([cat([gh[i], u_loc[i]], axis=1), cat([jnp.zeros_like(ins[i][3]), ins[i][3]], axis=1)], axis=0), _TN)
          for i in each]
    y_loc = [z[:, :gn] for z in yq]
    q = [pre[i][1] + yq[i][:, gn:] for i in each]
    mt = [_diag_blocks(ms[i][:, :gn], n) + eye_state * pre[i][6] for i in each]
    sloc = [_diag_blocks(z[:, gn:], n) for z in ms]
    return y_loc, q, mt, sloc


def _wkv_kernel(pr_ref, pk_ref, pv_ref, cr_ref, ck_ref, cv_ref, ar_ref, ak_ref, av_ref, mur_ref, muk_ref, muv_ref,
                tw_ref, al_ref, sg_ref, w0_ref, a0_ref, kkw_ref, kaw_ref, w2_ref, a2_ref, g2_ref, ind_ref, tri_ref,
                rk_ref, lng_ref, lnb_ref, s0_ref, o_ref, st_ref, state_ref, *, chunk):
    n = HEAD_SIZE
    nb, lt, width = pr_ref.shape
    rows = nb * lt
    gn = (LANES // chunk) * n
    tstep = pl.program_id(2)

    @pl.when(tstep == 0)
    def _():
        state_ref[...] = s0_ref[...]

    first = (lax.broadcasted_iota(jnp.int32, (rows, width), 0) % lt) == 0

    def shifted(x_ref, carried_ref, above_ref, mu_ref):
        x = x_ref[...].reshape(rows, width)
        if nb == 1:
            above = above_ref[0, SUBLANES - 1:, :]
            bnd = jnp.broadcast_to(jnp.where(tstep == 0, carried_ref[0], above), (rows, width))
        else:
            bnd = jnp.broadcast_to(carried_ref[...], (nb, lt, width)).reshape(rows, width)
        prev = jnp.where(first, bnd, pltpu.roll(x, 1, 0))
        return x + (prev - x) * mu_ref[...]

    flat = lambda ref: ref[...].reshape(rows, ref.shape[2])
    r, cum, k, v, kk, a, g = _rwkv_inputs(
        shifted(pr_ref, cr_ref, ar_ref, mur_ref), shifted(pk_ref, ck_ref, ak_ref, muk_ref),
        shifted(pv_ref, cv_ref, av_ref, muv_ref), flat(tw_ref), flat(al_ref), flat(sg_ref),
        w0_ref[...], a0_ref[...], kkw_ref[...], kaw_ref[...], w2_ref[...], a2_ref[...], g2_ref[...],
        ind_ref[...], tri_ref[...])
    bonus_w = r * k * rk_ref[...]

    mrow = lax.broadcasted_iota(jnp.int32, (chunk, LANES), 0)
    mcol = lax.broadcasted_iota(jnp.int32, (chunk, LANES), 1) % chunk
    srow = lax.broadcasted_iota(jnp.int32, (n, gn), 0)
    scol = lax.broadcasted_iota(jnp.int32, (n, gn), 1) % n
    masks = (mrow > mcol, mrow >= mcol, (srow == scol).astype(f32))
    head_ind = ind_ref[0:gn, 0:gn]

    chains = [(bi, p) for bi in range(nb) for p in range(width // gn)]
    nchunk = lt // chunk
    probs = [(bi, p, ci) for bi, p in chains for ci in range(nchunk)]
    where = {pr: i for i, pr in enumerate(probs)}

    def index(bi, p, ci):
        return bi, slice(ci * chunk, (ci + 1) * chunk), slice(p * gn, (p + 1) * gn)

    def at(x, bi, p, ci):
        return x[bi * lt + ci * chunk:bi * lt + (ci + 1) * chunk, p * gn:(p + 1) * gn]

    ins = [tuple(at(x, *pr) for x in (r, cum, k, v, kk, a)) for pr in probs]
    y_loc, q, mt, sloc = _wkv_local(ins, masks, chunk)

    st = {ch: state_ref[ch] for ch in chains}
    ys, stats = {}, {}
    for step in range(nchunk + 2):
        if step < nchunk:
            for ch in chains:
                i = where[ch + (step,)]
                mt_hi, mt_lo = _split(mt[i])
                r2 = lax.dot_general(jnp.concatenate([q[i].astype(bf16), mt_hi, mt_lo], axis=0),
                                     _bd(st[ch], n).astype(bf16), _NN, preferred_element_type=f32)
                ys[ch, step] = r2[:chunk] + y_loc[i]
                st[ch] = r2[chunk:chunk + n] + r2[chunk + n:] + sloc[i]
        if 0 <= step - 1 < nchunk:
            parts = []
            for ch in chains:
                y_hi, y_lo = _split(ys[ch, step - 1])
                parts += [y_hi, y_lo, at(bonus_w, *ch, step - 1).astype(bf16)]
            sums = lax.dot_general(jnp.concatenate(parts, axis=0), head_ind, _NN, preferred_element_type=f32)
            for ci, ch in enumerate(chains):
                stats[ch, step - 1] = sums[3 * chunk * ci:3 * chunk * (ci + 1)]
        if 0 <= step - 2 < nchunk:
            dlts = []
            for ch in chains:
                sm = stats[ch, step - 2]
                dlts.append(ys.pop((ch, step - 2)) - (sm[:chunk] + sm[chunk:2 * chunk]) * (1.0 / n))
            sq = jnp.concatenate([(x * x).astype(bf16) for x in dlts], axis=0)
            var = lax.dot_general(sq, head_ind, _NN, preferred_element_type=f32) * (1.0 / n)
            for ci, ch in enumerate(chains):
                idx = index(*ch, step - 2)
                lanes = idx[2]
                sm = stats.pop((ch, step - 2))
                out = (dlts[ci] * lax.rsqrt(var[chunk * ci:chunk * (ci + 1)] + GN_EPS) * lng_ref[:, lanes]
                       + lnb_ref[:, lanes] + sm[2 * chunk:] * at(v, *ch, step - 2))
                o_ref[idx] = (out * at(g, *ch, step - 2)).astype(o_ref.dtype)
    for ch in chains:
        state_ref[ch] = st[ch]

    @pl.when(tstep == pl.num_programs(2) - 1)
    def _():
        st_ref[...] = state_ref[...]


def _wkv(pa_rkv, carried, mu_rkv, lora_acts, w, s0, chunk, rows_per_step, seq_per_step, lanes_per_step):
    bsz, t, c3 = pa_rkv.shape
    c = c3 // 3
    n = HEAD_SIZE
    grp = LANES // chunk
    gn = grp * n
    heads = c // n
    nb, lt, width = seq_per_step, rows_per_step, lanes_per_step
    pg = width // gn
    nj = c // width
    s0k = s0.reshape(bsz, heads // grp, grp, n, n).transpose(0, 1, 4, 2, 3).reshape(bsz, heads // grp, n, gn)
    above = lambda ti: jnp.maximum(ti * (lt // SUBLANES) - 1, 0)
    tok = lambda off: pl.BlockSpec((nb, lt, width), lambda b, gi, ti: (b, ti, gi + off))
    car = lambda off: pl.BlockSpec((nb, 1, width), lambda b, gi, ti: (b, 0, gi + off))
    abv = lambda off: pl.BlockSpec((nb, SUBLANES, width), lambda b, gi, ti: (b, above(ti), gi + off))
    vec = lambda off: pl.BlockSpec((1, width), lambda b, gi, ti: (0, gi + off))
    act = lambda z: pl.BlockSpec((nb, lt, z.shape[2]), lambda b, gi, ti: (b, ti, 0))
    low = lambda z: pl.BlockSpec((z.shape[0], width), lambda b, gi, ti: (0, gi))
    const = lambda z: pl.BlockSpec(z.shape, lambda b, gi, ti: (0, 0))
    out_tok = pl.BlockSpec((nb, lt, width), lambda b, gi, ti: (b, ti, gi))
    st = pl.BlockSpec((nb, pg, n, gn), lambda b, gi, ti: (b, gi, 0, 0))
    ind, tri = _head_indicator(width), _chunk_tri(nb * lt, chunk)
    out, stk = pl.pallas_call(
        functools.partial(_wkv_kernel, chunk=chunk),
        out_shape=(jax.ShapeDtypeStruct((bsz, t, c), bf16), jax.ShapeDtypeStruct(s0k.shape, f32)),
        grid=(bsz // nb, nj, t // lt),
        in_specs=[tok(0), tok(nj), tok(2 * nj), car(0), car(nj), car(2 * nj), abv(0), abv(nj), abv(2 * nj),
                  vec(0), vec(nj), vec(2 * nj)] + [act(z) for z in lora_acts]
                 + [vec(0)] * 4 + [low(w["w2"]), low(w["a2"]), low(w["g2"]), const(ind), const(tri)]
                 + [vec(0)] * 3 + [st],
        out_specs=(out_tok, st),
        scratch_shapes=[pltpu.VMEM((nb, pg, n, gn), f32)],
        compiler_params=_params("parallel", "parallel", "arbitrary"),
        name="wkv",
    )(pa_rkv, pa_rkv, pa_rkv, carried, carried, carried, pa_rkv, pa_rkv, pa_rkv, mu_rkv, mu_rkv, mu_rkv,
      *lora_acts, w["w0"], w["a0"], w["k_k"], w["k_a"], w["w2"], w["a2"], w["g2"], ind, tri,
      w["r_k"], w["lnx_g"], w["lnx_b"], s0k)
    s_new = stk.reshape(bsz, heads // grp, n, grp, n).transpose(0, 1, 3, 4, 2).reshape(bsz, heads, n, n)
    return out, s_new


def _conv_kernel(u_ref, halo_ref, st_ref, w_ref, b_ref, o_ref, ext_ref, *, taps, sub_rows):
    nb, tt, _ = u_ref.shape
    first = pl.program_id(1) == 0

    @pl.when(first)
    def _():
        ext_ref[:, 0:CONV_HALO, :] = st_ref[...]

    @pl.when(jnp.logical_not(first))
    def _():
        ext_ref[:, 0:CONV_HALO, :] = halo_ref[...]

    ext_ref[:, CONV_HALO:CONV_HALO + tt, :] = u_ref[...]
    lead = CONV_HALO - (taps - 1)
    for bi in range(nb):
        for r0 in range(0, tt, sub_rows):
            acc = None
            for b in range(8):
                rows = sub_rows + (8 if b else 0)
                z = None
                for a8 in range(0, CONV_HALO + 8, 8):
                    tap = a8 + b - lead
                    if 0 <= tap < taps:
                        term = w_ref[tap:tap + 1, :] * ext_ref[bi, r0 + a8:r0 + a8 + rows, :]
                        z = term if z is None else z + term
                if z is not None:
                    z = z[b:b + sub_rows] if b else z
                    acc = z if acc is None else acc + z
            o_ref[bi, r0:r0 + sub_rows, :] = acc + b_ref[...]


def _conv(u, hist, dw_w, dw_b, seq_per_tile, rows_per_tile):
    bsz, t, c = u.shape
    nb, tt = seq_per_tile, rows_per_tile
    tc = _tile(c, 128)
    hpt = tt // CONV_HALO
    taps = dw_w.shape[0]
    return pl.pallas_call(
        functools.partial(_conv_kernel, taps=taps, sub_rows=_tile(tt, 128)),
        out_shape=jax.ShapeDtypeStruct((bsz, t, c), f32),
        grid=(bsz // nb, t // tt, c // tc),
        in_specs=[pl.BlockSpec((nb, tt, tc), lambda b, i, j: (b, i, j)),
                  pl.BlockSpec((nb, CONV_HALO, tc), lambda b, i, j: (b, jnp.maximum(i * hpt - 1, 0), j)),
                  pl.BlockSpec((nb, CONV_HALO, tc), lambda b, i, j: (b, 0, j)),
                  pl.BlockSpec((taps, tc), lambda b, i, j: (0, j)),
                  pl.BlockSpec((1, tc), lambda b, i, j: (0, j))],
        out_specs=pl.BlockSpec((nb, tt, tc), lambda b, i, j: (b, i, j)),
        scratch_shapes=[pltpu.VMEM((nb, CONV_HALO + tt, tc), f32)],
        compiler_params=_params("parallel", "arbitrary", "arbitrary"),
        name="dwconv",
    )(u, u, hist, dw_w, dw_b)


def _ln_silu_kernel(x_ref, g_ref, b_ref, o_ref):
    x = x_ref[...]
    mu = jnp.mean(x, axis=-1, keepdims=True)
    d = x - mu
    var = jnp.mean(d * d, axis=-1, keepdims=True)
    y = d * lax.rsqrt(var + LN_EPS) * g_ref[...] + b_ref[...]
    o_ref[...] = (y * _sigmoid(y)).astype(o_ref.dtype)


def _ln_silu(x, g, b):
    m, d = x.shape
    tm = _tile(m, 512)
    return pl.pallas_call(
        _ln_silu_kernel,
        out_shape=jax.ShapeDtypeStruct((m, d), bf16),
        grid=(m // tm,),
        in_specs=[pl.BlockSpec((tm, d), lambda i: (i, 0)), pl.BlockSpec((1, d), lambda i: (0, 0)),
                  pl.BlockSpec((1, d), lambda i: (0, 0))],
        out_specs=pl.BlockSpec((tm, d), lambda i: (i, 0)),
        compiler_params=_params("parallel"),
        name="ln_silu",
    )(x, g, b)


def _layer(x, p, s_wkv, s_shift, s_conv, w, *, chunk):
    bsz, t, d = x.shape
    m = bsz * t
    c = w["lnx_g"].shape[1]
    c3 = 3 * c
    n_shift = s_shift.shape[1]
    n_lora = n_shift - c3
    x2 = x.reshape(m, d)

    h = _rms_cast(x2, w["g_pre_mix"])
    lw_pad = -(-n_lora // 128) * 128
    wt = w["w_in_t"]
    pa_rkv = _matmul(h, wt, c3, name="mm_rkv")
    pa_lora = _matmul(h, wt, lw_pad, row0=c3, tm_pref=512, tn_pref=lw_pad, name="mm_lora")
    u = _matmul_glu(h, wt, c, row0=n_shift)
    gates = _matmul(h, wt, 2 * d, row0=n_shift + 2 * c, act="sigmoid", name="mm_gate")

    carried_lora = jnp.pad(s_shift[:, c3:], ((0, 0), (0, lw_pad - n_lora)))
    lora_widths = (w["w2"].shape[0], w["a2"].shape[0], w["g2"].shape[0])
    lora_acts = _lora_act(pa_lora, carried_lora[:, None, :], w["mu_lora"], lora_widths, t)
    wkv_rows = _tile(t, WKV_ROWS)
    group_lanes = (LANES // chunk) * HEAD_SIZE
    wkv_lanes = _tile(c, max(WKV_LANES if wkv_rows > chunk else WKV_LANES // 2, group_lanes))
    problems = (wkv_rows // chunk) * (wkv_lanes // group_lanes)
    ya_in, s_new = _wkv(pa_rkv.reshape(bsz, t, c3), s_shift[:, None, :c3], w["mu_rkv"],
                        [z.reshape(bsz, t, -1) for z in lora_acts], w, s_wkv, chunk=chunk,
                        rows_per_step=wkv_rows, seq_per_step=_tile(bsz, max(1, WKV_PROBLEMS // problems)),
                        lanes_per_step=wkv_lanes)

    last_rows = lambda z, width: lax.slice(z, (t - 1, 0), (m, width), (t, 1))
    shift_new = jnp.concatenate([last_rows(pa_rkv, c3), last_rows(pa_lora, n_lora)], axis=1)

    u3 = u.reshape(bsz, t, c)
    hist = jnp.pad(s_conv, ((0, 0), (CONV_HALO - s_conv.shape[1], 0), (0, 0)))
    conv_rows = _tile(t, 512)
    cv = _conv(u3, hist, w["dw_w"], w["dw_b"], _tile(bsz, max(1, 512 // conv_rows)), conv_rows)
    cv = _ln_silu(cv.reshape(m, c), w["lnc_g"], w["lnc_b"])
    keep = s_conv.shape[1]
    conv_new = jnp.concatenate([s_conv, u3], axis=1)[:, -keep:] if t < keep else u3[:, t - keep:]

    merged = _matmul_merge(ya_in.reshape(m, c), cv, w["w_o_a"], w["w_o_b"], gates)
    x1, hf = _out_proj(merged, w["w_out"], x2, w["g_post_mix"], w["g_pre_ffn"])
    x2_, hp = _ffn(hf, w["w_up"], w["w_down"], x1, w["g_post_ffn"], w["g_pre_ple"])
    y = _ple(hp, w["w_ple_gate"], p.reshape(m, p.shape[-1]), w["w_ple"], x2_, w["g_post_ple"])
    return y.reshape(bsz, t, d), s_new, shift_new, conv_new


def _layer_weights(i, c, g_pre_mix, w_in, mu_shift, w0, w2, a0, a2, g2, k_k, k_a, r_k, lnx_g, lnx_b, w_o_a,
                   dw_w, dw_b, lnc_g, lnc_b, w_o_b, w_out, g_post_mix, g_pre_ffn, w_up, w_down, g_post_ffn,
                   g_pre_ple, w_ple_gate, w_ple, g_post_ple):
    c3 = 3 * c
    n_lora = w2.shape[1] + a2.shape[1] + g2.shape[1]
    n_shift = c3 + n_lora
    lw_pad = -(-n_lora // 128) * 128
    row = lambda z: z[i].reshape(1, -1)
    win = w_in[i]
    return {
        "g_pre_mix": row(g_pre_mix),
        "w_in_t": jnp.swapaxes(win, 0, 1),
        "mu_rkv": mu_shift[i, :c3].reshape(1, -1),
        "mu_lora": jnp.pad(mu_shift[i, c3:], (0, lw_pad - n_lora)).reshape(1, -1),
        "w0": row(w0), "a0": row(a0), "k_k": row(k_k), "k_a": row(k_a),
        "w2": w2[i].astype(bf16), "a2": a2[i].astype(bf16), "g2": g2[i].astype(bf16),
        "r_k": row(r_k), "lnx_g": row(lnx_g), "lnx_b": row(lnx_b),
        "w_o_a": w_o_a[i].astype(bf16), "w_o_b": w_o_b[i].astype(bf16), "w_out": w_out[i].astype(bf16),
        "dw_w": dw_w[i], "dw_b": row(dw_b), "lnc_g": row(lnc_g), "lnc_b": row(lnc_b),
        "g_post_mix": row(g_post_mix), "g_pre_ffn": row(g_pre_ffn),
        "w_up": w_up[i].astype(bf16), "w_down": w_down[i].astype(bf16),
        "g_post_ffn": row(g_post_ffn), "g_pre_ple": row(g_pre_ple),
        "w_ple_gate": w_ple_gate[i].astype(bf16), "w_ple": w_ple[i].astype(bf16),
        "g_post_ple": row(g_post_ple),
    }


def kernel(x_prompt, x_sample, p_prompt, p_sample, state_wkv, state_shift, state_conv, g_pre_mix, w_in, mu_shift, w0, w2, a0, a2, g2, k_k, k_a, r_k, lnx_g, lnx_b, w_o_a, dw_w, dw_b, lnc_g, lnc_b, w_o_b, w_out, g_post_mix, g_pre_ffn, w_up, w_down, g_post_ffn, g_pre_ple, w_ple_gate, w_ple, g_post_ple):
    depth = w_in.shape[0]
    c = w_o_a.shape[1]
    heads = c // HEAD_SIZE
    bp, tp, _ = x_prompt.shape
    ts = x_sample.shape[1]
    xp, xs = x_prompt, x_sample
    outs = [[] for _ in range(6)]
    for i in range(depth):
        w = _layer_weights(i, c, g_pre_mix, w_in, mu_shift, w0, w2, a0, a2, g2, k_k, k_a, r_k, lnx_g, lnx_b,
                           w_o_a, dw_w, dw_b, lnc_g, lnc_b, w_o_b, w_out, g_post_mix, g_pre_ffn, w_up, w_down,
                           g_post_ffn, g_pre_ple, w_ple_gate, w_ple, g_post_ple)
        xp, s1, s2, s3 = _layer(xp, p_prompt[i],
                                jnp.zeros((bp, heads, HEAD_SIZE, HEAD_SIZE), f32),
                                jnp.zeros((bp, state_shift.shape[2]), f32),
                                jnp.zeros((bp,) + state_conv.shape[2:], f32), w,
                                chunk=min(64, tp))
        xs, t1, t2, t3 = _layer(xs, p_sample[i], state_wkv[i], state_shift[i], state_conv[i], w,
                                chunk=min(64, ts))
        for lst, val in zip(outs, (s1, s2, s3, t1, t2, t3)):
            lst.append(val)
    return (xp, xs) + tuple(jnp.stack(o) for o in outs)
```
